```python
import jax, jax.numpy as jnp
from jax import lax
import numpy as np

D_MODEL = 2048
BATCH = 4
SEQ = 4096
DEPTH = 2

GRID_W = 64
CTX_LEN = 256
N_MIXERS = 2
MLSTM_HEADS = 8
MLSTM_DV = D_MODEL // MLSTM_HEADS
MLSTM_DK = MLSTM_DV // 2
MLSTM_CHUNK = 64
MLSTM_STATE_COLS = 2 * MLSTM_HEADS * MLSTM_DK + MLSTM_HEADS * MLSTM_DV + 4 * MLSTM_HEADS
MLSTM_IN_COLS = MLSTM_STATE_COLS + MLSTM_HEADS * MLSTM_DV
CONV_WIDTH = 3
D_FF = (((8 * D_MODEL) // 3 + 127) // 128) * 128
N_MOD = 9
RMS_EPS = 1e-6
M_INIT = -1e30

kernel_name = "hybrid_mlstm_shortconv_macaron_dit"


def _rmsnorm(x, g):
    xf = x.astype(jnp.float32)
    y = xf * lax.rsqrt(jnp.mean(xf * xf, axis=-1, keepdims=True) + RMS_EPS)
    return (y * g.astype(jnp.float32)).astype(x.dtype)


def _modulate(x, g, shift, scale):
    return _rmsnorm(x, g) * (1.0 + scale) + shift


def _mod_parts(mod, j):
    return mod[..., 3 * j, :], mod[..., 3 * j + 1, :], mod[..., 3 * j + 2, :]


def _swiglu(h, w_in, w_out):
    g, u = jnp.split(h @ w_in, 2, axis=-1)
    return (jax.nn.silu(g) * u) @ w_out


def _mlstm_split(proj, b_gate):
    B, T, cols = proj.shape
    H, DK, DV = MLSTM_HEADS, MLSTM_DK, MLSTM_DV
    heads = lambda a, d: a.reshape(B, T, H, d).transpose(0, 2, 1, 3)
    o0, o1, o2, o3 = H * DK, 2 * H * DK, 2 * H * DK + H * DV, MLSTM_STATE_COLS
    q = heads(proj[..., :o0], DK) * (DK ** -0.5)
    k = heads(proj[..., o0:o1], DK)
    v = heads(proj[..., o1:o2], DV)
    gates = (proj[..., o2:o3] + b_gate.astype(jnp.float32)).reshape(B, T, 4, H).transpose(2, 0, 3, 1)
    o = proj[..., o3:] if cols == MLSTM_IN_COLS else None
    return q, k, v, gates, o


def _mlstm_scan(q, k, v, log_i, log_f, state0, return_h):
    B, H, T, DK = q.shape
    DV = v.shape[-1]
    L = MLSTM_CHUNK
    nc = T // L

    def chunks(a):
        return jnp.moveaxis(a.reshape(a.shape[:2] + (nc, L) + a.shape[3:]), 2, 0)

    xs = (chunks(q), chunks(k), chunks(v), chunks(log_i), chunks(log_f))
    causal = jnp.tril(jnp.ones((L, L), dtype=bool))

    def step(carry, xc):
        C, n, m = carry
        qc, kc, vc, ic, fc = xc
        b = jnp.cumsum(fc, axis=-1)
        b_end = b[..., -1]
        w_end = b_end[..., None] - b + ic
        m_new = jnp.maximum(b_end + m, jnp.max(w_end, axis=-1))
        carry_decay = jnp.exp(b_end + m - m_new)
        w = jnp.exp(w_end - m_new[..., None])[..., None]
        C_new = carry_decay[..., None, None] * C + jnp.einsum('bhld,bhle->bhde', kc * w, vc)
        n_new = carry_decay[..., None] * n + jnp.sum(kc * w, axis=2)
        if not return_h:
            return (C_new, n_new, m_new), None
        d_log = jnp.where(causal, b[..., :, None] - b[..., None, :] + ic[..., None, :], -jnp.inf)
        inter = b + m[..., None]
        m_out = jnp.maximum(inter, jnp.max(d_log, axis=-1))
        s = jnp.einsum('bhsd,bhjd->bhsj', qc, kc) * jnp.exp(d_log - m_out[..., None])
        w_inter = jnp.exp(inter - m_out)[..., None]
        num = jnp.einsum('bhsj,bhje->bhse', s, vc) + w_inter * jnp.einsum('bhsd,bhde->bhse', qc, C)
        den = jnp.sum(s, axis=-1, keepdims=True) + w_inter * jnp.einsum('bhsd,bhd->bhs', qc, n)[..., None]
        h = num / jnp.maximum(jnp.abs(den), jnp.exp(-m_out)[..., None])
        return (C_new, n_new, m_new), h

    state, hs = lax.scan(step, state0, xs)
    if not return_h:
        return None, state
    return jnp.moveaxis(hs, 0, 2).reshape(B, H, T, DV), state


def _mlstm_bidir(q, k, v, gates, state_f0, state_b0, return_h):
    i_f, f_f = gates[0], jax.nn.log_sigmoid(gates[1])
    i_b, f_b = gates[2], jax.nn.log_sigmoid(gates[3])
    flip = lambda a: jnp.flip(a, axis=2)
    h_f, st_f = _mlstm_scan(q, k, v, i_f, f_f, state_f0, return_h)
    h_b, st_b = _mlstm_scan(flip(q), flip(k), flip(v), flip(i_b), flip(f_b), state_b0, return_h)
    h = h_f + flip(h_b) if return_h else None
    return h, st_f, st_b


def _mlstm_out(h, o, norm_g, w_out, dtype):
    B, H, T, DV = h.shape
    h = h.transpose(0, 2, 1, 3)
    hn = h * lax.rsqrt(jnp.mean(h * h, axis=-1, keepdims=True) + RMS_EPS)
    hn = hn * norm_g.astype(jnp.float32).reshape(H, DV)
    y = hn * jax.nn.sigmoid(o.reshape(B, T, H, DV))
    return y.reshape(B, T, H * DV).astype(dtype) @ w_out


def _zero_state(batch):
    H, DK, DV = MLSTM_HEADS, MLSTM_DK, MLSTM_DV
    return (jnp.zeros((batch, H, DK, DV), jnp.float32),
            jnp.zeros((batch, H, DK), jnp.float32),
            jnp.full((batch, H), M_INIT, jnp.float32))


def _conv3(z, w):
    pad = [(0, 0)] * (z.ndim - 2) + [(1, 1), (0, 0)]
    zp = jnp.pad(z, pad)
    return w[0] * zp[..., :-2, :] + w[1] * zp[..., 1:-1, :] + w[2] * zp[..., 2:, :]


def _short_conv(h, w_in, w_conv, w_out, on_grid):
    bg, cg, u = jnp.split(h @ w_in, 3, axis=-1)
    z = cg * u
    if on_grid:
        B, T, D = z.shape
        rows = T // GRID_W
        zc = _conv3(z.reshape(B, rows, GRID_W, D), w_conv).reshape(B, T, D)
    else:
        zc = _conv3(z, w_conv)
    return (bg * zc) @ w_out


def setup_inputs(seed: int = 0) -> dict:
    key = jax.random.key(seed)
    ks = jax.random.split(key, 24)
    D, H = D_MODEL, MLSTM_HEADS
    n_a = len(range(0, DEPTH, N_MIXERS))
    n_b = len(range(1, DEPTH, N_MIXERS))
    nrm = lambda k, shape, fan_in: jax.random.normal(k, shape, jnp.float32) * (fan_in ** -0.5)
    i_bias = 0.5 * jax.random.normal(ks[10], (n_a, 2, H), jnp.float32)
    f_bias = jax.random.uniform(ks[11], (n_a, 2, H), jnp.float32, minval=3.0, maxval=6.0)
    b_gate = jnp.stack([i_bias[:, 0], f_bias[:, 0], i_bias[:, 1], f_bias[:, 1]], axis=1).reshape(n_a, 4 * H)
    return {
        "x": jax.random.normal(ks[0], (BATCH, SEQ, D), jnp.float32),
        "c": jax.random.normal(ks[1], (BATCH, D), jnp.float32),
        "ctx": jax.random.normal(ks[2], (BATCH, CTX_LEN, D), jnp.float32),
        "c_ctx": jax.random.normal(ks[3], (D,), jnp.float32),
        "w_mod": nrm(ks[4], (DEPTH, D, N_MOD * D), D),
        "b_mod": 0.02 * jax.random.normal(ks[5], (DEPTH, N_MOD * D), jnp.float32),
        "norm_g": 1.0 + 0.1 * jax.random.normal(ks[6], (DEPTH, 3, D), jnp.float32),
        "ffn_w_in": nrm(ks[7], (DEPTH, 2, D, 2 * D_FF), D),
        "ffn_w_out": nrm(ks[8], (DEPTH, 2, D_FF, D), D_FF),
        "mlstm_w_in": nrm(ks[9], (n_a, D, MLSTM_IN_COLS), D),
        "mlstm_b_gate": b_gate,
        "mlstm_norm_g": 1.0 + 0.1 * jax.random.normal(ks[12], (n_a, H * MLSTM_DV), jnp.float32),
        "mlstm_w_out": nrm(ks[13], (n_a, H * MLSTM_DV, D), H * MLSTM_DV),
        "conv_w_in": nrm(ks[14], (n_b, D, 3 * D), D),
        "conv_w": nrm(ks[15], (n_b, CONV_WIDTH, D), CONV_WIDTH),
        "conv_w_out": nrm(ks[16], (n_b, D, D), D),
        "final_norm_g": 1.0 + 0.1 * jax.random.normal(ks[17], (D,), jnp.float32),
    }


def reference(x, c, ctx, c_ctx, w_mod, b_mod, norm_g, ffn_w_in, ffn_w_out,
              mlstm_w_in, mlstm_b_gate, mlstm_norm_g, mlstm_w_out,
              conv_w_in, conv_w, conv_w_out, final_norm_g):
    batch = x.shape[0]
    for i in range(DEPTH):
        kind = i % N_MIXERS
        slot = i // N_MIXERS
        ctx_after = any(j % N_MIXERS == 0 for j in range(i + 1, DEPTH))
        ctx_here = (kind == 0) or ctx_after
        mod = (jax.nn.silu(c) @ w_mod[i] + b_mod[i]).reshape(batch, 1, N_MOD, D_MODEL)
        if ctx_here:
            mod_c = (jax.nn.silu(c_ctx) @ w_mod[i] + b_mod[i]).reshape(N_MOD, D_MODEL)

        sh, sc, g = _mod_parts(mod, 0)
        x = x + 0.5 * g * _swiglu(_modulate(x, norm_g[i, 0], sh, sc), ffn_w_in[i, 0], ffn_w_out[i, 0])
        if ctx_here:
            sh_c, sc_c, g_c = _mod_parts(mod_c, 0)
            ctx = ctx + 0.5 * g_c * _swiglu(_modulate(ctx, norm_g[i, 0], sh_c, sc_c), ffn_w_in[i, 0], ffn_w_out[i, 0])

        sh, sc, g = _mod_parts(mod, 1)
        hx = _modulate(x, norm_g[i, 1], sh, sc)
        if ctx_here:
            sh_c, sc_c, g_c = _mod_parts(mod_c, 1)
            hc = _modulate(ctx, norm_g[i, 1], sh_c, sc_c)
        if kind == 0:
            w_in = mlstm_w_in[slot]
            c_cols = MLSTM_IN_COLS if ctx_after else MLSTM_STATE_COLS
            q_c, k_c, v_c, gt_c, o_c = _mlstm_split((hc @ w_in[:, :c_cols]).astype(jnp.float32), mlstm_b_gate[slot])
            h_c, st_f, st_b = _mlstm_bidir(q_c, k_c, v_c, gt_c, _zero_state(batch), _zero_state(batch), ctx_after)
            q_x, k_x, v_x, gt_x, o_x = _mlstm_split((hx @ w_in).astype(jnp.float32), mlstm_b_gate[slot])
            h_x, _, _ = _mlstm_bidir(q_x, k_x, v_x, gt_x, st_f, st_b, True)
            x = x + g * _mlstm_out(h_x, o_x, mlstm_norm_g[slot], mlstm_w_out[slot], x.dtype)
            if ctx_after:
                ctx = ctx + g_c * _mlstm_out(h_c, o_c, mlstm_norm_g[slot], mlstm_w_out[slot], ctx.dtype)
        else:
            x = x + g * _short_conv(hx, conv_w_in[slot], conv_w[slot], conv_w_out[slot], True)
            if ctx_after:
                ctx = ctx + g_c * _short_conv(hc, conv_w_in[slot], conv_w[slot], conv_w_out[slot], False)

        sh, sc, g = _mod_parts(mod, 2)
        x = x + 0.5 * g * _swiglu(_modulate(x, norm_g[i, 2], sh, sc), ffn_w_in[i, 1], ffn_w_out[i, 1])
        if ctx_after:
            sh_c, sc_c, g_c = _mod_parts(mod_c, 2)
            ctx = ctx + 0.5 * g_c * _swiglu(_modulate(ctx, norm_g[i, 2], sh_c, sc_c), ffn_w_in[i, 1], ffn_w_out[i, 1])
    return _rmsnorm(x, final_norm_g)
```

```python
import functools

import jax
import jax.numpy as jnp
from jax import lax
from jax.experimental import pallas as pl
from jax.experimental.pallas import tpu as pltpu

F32 = jnp.float32
BF16 = jnp.bfloat16

HEADS = 8
N_MOD = 9
GRID_W = 64
RMS_EPS = 1e-6
M_INIT = -1e30

LANES = 128
MXU_COLS = 256
VMEM_LIMIT_BYTES = 56 * 1024 * 1024

NORM_ROWS = 256
FFN_TF = 512
CONV_TN = 512
PROJ_TN = 512
SCAN_L = 256
MOD_TN = 1024


def _params(sem):
    return pltpu.CompilerParams(dimension_semantics=sem, vmem_limit_bytes=VMEM_LIMIT_BYTES)


def _sigmoid(x):
    return 1.0 / (1.0 + jnp.exp(-x))


def _norm_mod(x, g, shift, scale):
    ms = jnp.mean(x * x, axis=-1, keepdims=True)
    y = x * lax.rsqrt(ms + RMS_EPS) * g
    return y * (1.0 + scale) + shift


def _fill_xn(x_ref, xn_ref, g, mod):
    rows = x_ref.shape[0]
    rc = min(NORM_ROWS, rows)
    shift, scale = mod[0:1], mod[1:2]

    def body(r, carry):
        sl = pl.ds(pl.multiple_of(r * rc, rc), rc)
        xn_ref[sl, :] = _norm_mod(x_ref[sl, :], g, shift, scale).astype(BF16)
        return carry

    lax.fori_loop(0, rows // rc, body, 0)


def _mod_kernel(c_ref, w_ref, b_ref, o_ref):
    c = c_ref[...]
    s = (c * _sigmoid(c)).astype(BF16)
    w = w_ref[0].astype(BF16)
    o_ref[0] = jnp.dot(s, w, preferred_element_type=F32) + b_ref[0]


def _modulation(c_rows, w_mod, b_mod):
    depth, d, n = w_mod.shape
    r = c_rows.shape[0]
    return pl.pallas_call(
        _mod_kernel,
        grid=(depth, n // MOD_TN),
        in_specs=[
            pl.BlockSpec((r, d), lambda l, j: (0, 0)),
            pl.BlockSpec((1, d, MOD_TN), lambda l, j: (l, 0, j)),
            pl.BlockSpec((1, 1, MOD_TN), lambda l, j: (l, 0, j)),
        ],
        out_specs=pl.BlockSpec((1, r, MOD_TN), lambda l, j: (l, 0, j)),
        out_shape=jax.ShapeDtypeStruct((depth, r, n), F32),
        compiler_params=_params(("arbitrary", "arbitrary")),
        name="modulation",
    )(c_rows, w_mod, b_mod.reshape(depth, 1, n))


def _ffn_kernel(*refs, tf, final):
    if final:
        x_ref, g_ref, mod_ref, wgu_ref, wo_ref, fg_ref, o_ref, xn_ref = refs
    else:
        x_ref, g_ref, mod_ref, wgu_ref, wo_ref, o_ref, xn_ref = refs
    j = pl.program_id(1)
    mod = mod_ref[0]

    @pl.when(j == 0)
    def _():
        _fill_xn(x_ref, xn_ref, g_ref[...], mod)
        o_ref[...] = jnp.zeros_like(o_ref)

    h = jnp.dot(xn_ref[...], wgu_ref[...], preferred_element_type=F32)
    hg, hu = h[:, :tf], h[:, tf:]
    a = (hg * _sigmoid(hg) * hu).astype(BF16)
    o_ref[...] += jnp.dot(a, wo_ref[...], preferred_element_type=F32)

    @pl.when(j == pl.num_programs(1) - 1)
    def _():
        y = x_ref[...] + (0.5 * mod[2:3]) * o_ref[...]
        if final:
            ms = jnp.mean(y * y, axis=-1, keepdims=True)
            y = y * lax.rsqrt(ms + RMS_EPS) * fg_ref[...]
        o_ref[...] = y


def _ffn_weights(w_in, w_out):
    d, f2 = w_in.shape
    f = f2 // 2
    fp = -(-f // FFN_TF) * FFN_TF
    nj = fp // FFN_TF
    wg = jnp.pad(w_in[:, :f], ((0, 0), (0, fp - f))).astype(BF16).reshape(d, nj, 1, FFN_TF)
    wu = jnp.pad(w_in[:, f:], ((0, 0), (0, fp - f))).astype(BF16).reshape(d, nj, 1, FFN_TF)
    wgu = jnp.concatenate([wg, wu], axis=2).reshape(d, nj * 2 * FFN_TF)
    wo = jnp.pad(w_out, ((0, fp - f), (0, 0))).astype(BF16)
    return wgu, wo


def _ffn(x, g, mod3, wgu, wo, rows_per_mod, tm, final_g=None):
    m, d = x.shape
    fp = wo.shape[0]
    nj = fp // FFN_TF
    tpm = rows_per_mod // tm
    final = final_g is not None
    in_specs = [
        pl.BlockSpec((tm, d), lambda i, j: (i, 0)),
        pl.BlockSpec((1, d), lambda i, j: (0, 0)),
        pl.BlockSpec((1, 3, d), lambda i, j: (i // tpm, 0, 0)),
        pl.BlockSpec((d, 2 * FFN_TF), lambda i, j: (0, j)),
        pl.BlockSpec((FFN_TF, d), lambda i, j: (j, 0)),
    ]
    args = [x, g.reshape(1, d), mod3, wgu, wo]
    if final:
        in_specs.append(pl.BlockSpec((1, d), lambda i, j: (0, 0)))
        args.append(final_g.reshape(1, d))
    return pl.pallas_call(
        functools.partial(_ffn_kernel, tf=FFN_TF, final=final),
        grid=(m // tm, nj),
        in_specs=in_specs,
        out_specs=pl.BlockSpec((tm, d), lambda i, j: (i, 0)),
        out_shape=jax.ShapeDtypeStruct((m, d), F32),
        scratch_shapes=[pltpu.VMEM((tm, d), BF16)],
        compiler_params=_params(("parallel", "arbitrary")),
        name="ffn_final" if final else "ffn",
    )(*args)


def _proj_kernel(x_ref, g_ref, mod_ref, w_ref, wkt_ref, wg_ref, bg_ref,
                 main_ref, kt_ref, gates_ref, xn_ref):
    j = pl.program_id(1)

    @pl.when(j == 0)
    def _():
        _fill_xn(x_ref, xn_ref, g_ref[...], mod_ref[0])
        xn = xn_ref[...]
        kt = lax.dot_general(wkt_ref[...], xn, (((1,), (1,)), ((), ())),
                             preferred_element_type=F32)
        kt_ref[0] = kt.astype(BF16)
        gates_ref[...] = jnp.dot(xn, wg_ref[...], preferred_element_type=F32) + bg_ref[...]

    main_ref[...] = jnp.dot(xn_ref[...], w_ref[...], preferred_element_type=F32).astype(BF16)


def _proj(x, g, mod3, w_main, w_kt, w_gate, b_gate, rows_per_batch, tm):
    m, d = x.shape
    n = w_main.shape[1]
    nk = w_kt.shape[0]
    batch = m // rows_per_batch
    tpb = rows_per_batch // tm
    nmod = mod3.shape[0]
    mod_idx = (lambda i, j: (i // tpb, 0, 0)) if nmod > 1 else (lambda i, j: (0, 0, 0))
    return pl.pallas_call(
        _proj_kernel,
        grid=(m // tm, n // PROJ_TN),
        in_specs=[
            pl.BlockSpec((tm, d), lambda i, j: (i, 0)),
            pl.BlockSpec((1, d), lambda i, j: (0, 0)),
            pl.BlockSpec((1, 3, d), mod_idx),
            pl.BlockSpec((d, PROJ_TN), lambda i, j: (0, j)),
            pl.BlockSpec((nk, d), lambda i, j: (0, 0)),
            pl.BlockSpec((d, LANES), lambda i, j: (0, 0)),
            pl.BlockSpec((1, LANES), lambda i, j: (0, 0)),
        ],
        out_specs=[
            pl.BlockSpec((tm, PROJ_TN), lambda i, j: (i, j)),
            pl.BlockSpec((1, nk, tm), lambda i, j: (i // tpb, 0, i % tpb)),
            pl.BlockSpec((tm, LANES), lambda i, j: (i, 0)),
        ],
        out_shape=[
            jax.ShapeDtypeStruct((m, n), BF16),
            jax.ShapeDtypeStruct((batch, nk, rows_per_batch), BF16),
            jax.ShapeDtypeStruct((m, LANES), F32),
        ],
        scratch_shapes=[pltpu.VMEM((tm, d), BF16)],
        compiler_params=_params(("parallel", "arbitrary")),
        name="mlstm_proj",
    )(x, g.reshape(1, d), mod3, w_main, w_kt, w_gate, b_gate)


def _tri_sum(tri, x):
    x1 = x.astype(BF16)
    r1 = x - x1.astype(F32)
    x2 = r1.astype(BF16)
    x3 = (r1 - x2.astype(F32)).astype(BF16)
    dot = functools.partial(jnp.dot, preferred_element_type=F32)
    return dot(tri, x1) + dot(tri, x2) + dot(tri, x3)


def _gateprep_kernel(g_ref, pcol_ref, prow_ref, *, chunk):
    t = g_ref.shape[1]
    row = lax.broadcasted_iota(jnp.int32, (chunk, chunk), 0)
    col = lax.broadcasted_iota(jnp.int32, (chunk, chunk), 1)
    tril = jnp.where(row >= col, 1.0, 0.0).astype(BF16)
    triu = jnp.where(row <= col, 1.0, 0.0).astype(BF16)
    lane = lax.broadcasted_iota(jnp.int32, (chunk, LANES), 1)
    f_fwd = (lane >= HEADS) & (lane < 2 * HEADS)
    f_bwd = (lane >= 3 * HEADS) & (lane < 4 * HEADS)

    def body(ci, carry):
        sl = pl.ds(pl.multiple_of(ci * chunk, chunk), chunk)
        g = g_ref[0, sl, :]
        logf = jnp.minimum(g, 0.0) - jnp.log(1.0 + jnp.exp(-jnp.abs(g)))
        logf = jnp.where(f_fwd | f_bwd, logf, 0.0)
        p = jnp.where(f_fwd, _tri_sum(tril, logf), jnp.where(f_bwd, _tri_sum(triu, logf), g))
        pcol_ref[0, sl, :] = p
        prow_ref[0, :, sl] = p.T
        return carry

    lax.fori_loop(0, t // chunk, body, 0)


def _gateprep(gates, batch, chunk):
    m = gates.shape[0]
    t = m // batch
    g3 = gates.reshape(batch, t, LANES)
    return pl.pallas_call(
        functools.partial(_gateprep_kernel, chunk=chunk),
        grid=(batch,),
        in_specs=[pl.BlockSpec((1, t, LANES), lambda b: (b, 0, 0))],
        out_specs=[
            pl.BlockSpec((1, t, LANES), lambda b: (b, 0, 0)),
            pl.BlockSpec((1, LANES, t), lambda b: (b, 0, 0)),
        ],
        out_shape=[
            jax.ShapeDtypeStruct((batch, t, LANES), F32),
            jax.ShapeDtypeStruct((batch, LANES, t), F32),
        ],
        compiler_params=_params(("parallel",)),
        name="gate_prep",
    )(g3)


def _scan_kernel(q_ref, v_ref, o_ref, kt_ref, pcol_ref, prow_ref,
                 ktc_ref, vc_ref, pcolc_ref, prowc_ref, ng_ref,
                 y_ref, hf_ref, hb_ref, *, chunk, dk, dv):
    head = pl.program_id(1)
    t = q_ref.shape[1]
    tc = vc_ref.shape[1]
    nc = t // chunk
    ncc = tc // chunk
    lane = lax.broadcasted_iota(jnp.int32, (chunk, LANES), 1)
    row = lax.broadcasted_iota(jnp.int32, (chunk, chunk), 0)
    colm = lax.broadcasted_iota(jnp.int32, (chunk, chunk), 1)
    ones = jnp.ones((chunk, LANES), BF16)
    dot = functools.partial(jnp.dot, preferred_element_type=F32)

    def column(p, idx):
        return jnp.sum(jnp.where(lane == idx, p, 0.0), axis=1, keepdims=True)

    def gate_cols(pc_ref, t0, base):
        p = pc_ref[0, pl.ds(t0, chunk), :]
        return column(p, base + head), column(p, base + HEADS + head)

    def gate_rows(pr_ref, t0, base):
        i_row = pr_ref[0, pl.ds(base + head, 1), pl.ds(t0, chunk)]
        c_row = pr_ref[0, pl.ds(base + HEADS + head, 1), pl.ds(t0, chunk)]
        return i_row, c_row

    def update(kt, vaug, i_col, cum_col, tot, caug, m):
        w_end = tot + (i_col - cum_col)
        m_new = jnp.maximum(tot + m, jnp.max(w_end, axis=0, keepdims=True))
        decay = jnp.exp(tot + m - m_new)
        wv = (jnp.exp(w_end - m_new) * vaug.astype(F32)).astype(BF16)
        return decay * caug + dot(kt, wv), m_new

    def output(q, kt, vaug, i_col, cum_col, i_row, cum_row, mask, caug, m):
        s = dot(q, kt)
        dm = jnp.where(mask, cum_col + (i_row - cum_row), -jnp.inf)
        inter = cum_col + m
        m_out = jnp.maximum(inter, jnp.max(dm, axis=1, keepdims=True))
        p = (s * jnp.exp(dm - m_out)).astype(BF16)
        r = dot(p, vaug) + jnp.exp(inter - m_out) * dot(q, caug.astype(BF16))
        inv = 1.0 / jnp.maximum(jnp.abs(r[:, dv:]), jnp.exp(-m_out))
        return jnp.concatenate([r[:, i * LANES:(i + 1) * LANES] * inv for i in range(dv // LANES)], axis=1)

    def aug(v):
        return jnp.concatenate([v, ones], axis=1)

    caug0 = jnp.zeros((dk, dv + LANES), F32)
    m0 = jnp.full((1, 1), M_INIT, F32)
    cf, mf = caug0, m0
    for ci in range(ncc):
        t0 = ci * chunk
        i_col, b_col = gate_cols(pcolc_ref, t0, 0)
        cf, mf = update(ktc_ref[0, :, pl.ds(t0, chunk)], aug(vc_ref[0, pl.ds(t0, chunk), :]),
                        i_col, b_col, b_col[chunk - 1:chunk], cf, mf)
    cb, mb = caug0, m0
    for ci in reversed(range(ncc)):
        t0 = ci * chunk
        i_col, e_col = gate_cols(pcolc_ref, t0, 2 * HEADS)
        cb, mb = update(ktc_ref[0, :, pl.ds(t0, chunk)], aug(vc_ref[0, pl.ds(t0, chunk), :]),
                        i_col, e_col, e_col[0:1], cb, mb)

    def body(ci, carry):
        cf, mf, cb, mb = carry
        t0 = pl.multiple_of(ci * chunk, chunk)
        q = q_ref[0, pl.ds(t0, chunk), :]
        kt = kt_ref[0, :, pl.ds(t0, chunk)]
        vaug = aug(v_ref[0, pl.ds(t0, chunk), :])
        i_col, b_col = gate_cols(pcol_ref, t0, 0)
        i_row, b_row = gate_rows(prow_ref, t0, 0)
        hf_ref[pl.ds(t0, chunk), :] = output(q, kt, vaug, i_col, b_col, i_row, b_row, row >= colm, cf, mf)
        cf, mf = update(kt, vaug, i_col, b_col, b_col[chunk - 1:chunk], cf, mf)
        t1 = pl.multiple_of((nc - 1 - ci) * chunk, chunk)
        q = q_ref[0, pl.ds(t1, chunk), :]
        kt = kt_ref[0, :, pl.ds(t1, chunk)]
        vaug = aug(v_ref[0, pl.ds(t1, chunk), :])
        i_col, e_col = gate_cols(pcol_ref, t1, 2 * HEADS)
        i_row, e_row = gate_rows(prow_ref, t1, 2 * HEADS)
        hb_ref[pl.ds(t1, chunk), :] = output(q, kt, vaug, i_col, e_col, i_row, e_row, row <= colm, cb, mb)
        cb, mb = update(kt, vaug, i_col, e_col, e_col[0:1], cb, mb)
        return cf, mf, cb, mb

    lax.fori_loop(0, nc, body, (cf, mf, cb, mb))

    ng = ng_ref[...]

    def finish(ci, carry):
        sl = pl.ds(pl.multiple_of(ci * chunk, chunk), chunk)
        h = hf_ref[sl, :] + hb_ref[sl, :]
        hn = h * lax.rsqrt(jnp.mean(h * h, axis=1, keepdims=True) + RMS_EPS) * ng
        y_ref[0, sl, :] = (hn * _sigmoid(o_ref[0, sl, :].astype(F32))).astype(BF16)
        return carry

    lax.fori_loop(0, nc, finish, 0)


def _scan(main, kt, pcol, prow, vc, ktc, pcolc, prowc, norm_g, batch, t, tc, d):
    dv = d // HEADS
    dk = dv // 2
    qb = HEADS * dk // dk
    vb = HEADS * dk // dv
    ob = vb + HEADS
    main3 = main.reshape(batch, t, main.shape[1])
    vc3 = vc.reshape(batch, tc, vc.shape[1])
    del qb
    return pl.pallas_call(
        functools.partial(_scan_kernel, chunk=SCAN_L, dk=dk, dv=dv),
        grid=(batch, HEADS),
        in_specs=[
            pl.BlockSpec((1, t, dk), lambda b, h: (b, 0, h)),
            pl.BlockSpec((1, t, dv), lambda b, h: (b, 0, vb + h)),
            pl.BlockSpec((1, t, dv), lambda b, h: (b, 0, ob + h)),
            pl.BlockSpec((1, dk, t), lambda b, h: (b, h, 0)),
            pl.BlockSpec((1, t, LANES), lambda b, h: (b, 0, 0)),
            pl.BlockSpec((1, LANES, t), lambda b, h: (b, 0, 0)),
            pl.BlockSpec((1, dk, tc), lambda b, h: (b, h, 0)),
            pl.BlockSpec((1, tc, dv), lambda b, h: (b, 0, h)),
            pl.BlockSpec((1, tc, LANES), lambda b, h: (b, 0, 0)),
            pl.BlockSpec((1, LANES, tc), lambda b, h: (b, 0, 0)),
            pl.BlockSpec((1, dv), lambda b, h: (0, h)),
        ],
        out_specs=pl.BlockSpec((1, t, dv), lambda b, h: (b, 0, h)),
        out_shape=jax.ShapeDtypeStruct((batch, t, d), BF16),
        scratch_shapes=[pltpu.VMEM((t, dv), F32), pltpu.VMEM((t, dv), F32)],
        compiler_params=_params(("parallel", "arbitrary")),
        name="mlstm_scan",
    )(main3, main3, main3, kt, pcol, prow, ktc, vc3, pcolc, prowc, norm_g.reshape(1, d))


def _outproj_kernel(y_ref, x_ref, mod_ref, w_ref, o_ref):
    gate = mod_ref[0][2:3]
    o_ref[...] = x_ref[...] + gate * jnp.dot(y_ref[...], w_ref[...], preferred_element_type=F32)


def _outproj(y, x, mod3, w, rows_per_mod, tm):
    m, d = x.shape
    tpm = rows_per_mod // tm
    return pl.pallas_call(
        _outproj_kernel,
        grid=(m // tm,),
        in_specs=[
            pl.BlockSpec((tm, d), lambda i: (i, 0)),
            pl.BlockSpec((tm, d), lambda i: (i, 0)),
            pl.BlockSpec((1, 3, d), lambda i: (i // tpm, 0, 0)),
            pl.BlockSpec((d, d), lambda i: (0, 0)),
        ],
        out_specs=pl.BlockSpec((tm, d), lambda i: (i, 0)),
        out_shape=jax.ShapeDtypeStruct((m, d), F32),
        compiler_params=_params(("parallel",)),
        name="mlstm_out",
    )(y, x, mod3, w)


def _conv_kernel(x_ref, g_ref, mod_ref, w3_ref, cw_ref, wo_ref, o_ref, xn_ref, *, tn):
    j = pl.program_id(1)
    mod = mod_ref[0]
    tm = x_ref.shape[0]

    @pl.when(j == 0)
    def _():
        _fill_xn(x_ref, xn_ref, g_ref[...], mod)
        o_ref[...] = jnp.zeros_like(o_ref)

    h = jnp.dot(xn_ref[...], w3_ref[...], preferred_element_type=F32)
    bg, cg, u = h[:, :tn], h[:, tn:2 * tn], h[:, 2 * tn:]
    z = cg * u
    pos = lax.broadcasted_iota(jnp.int32, (tm, tn), 0) % GRID_W
    z_prev = jnp.where(pos == 0, 0.0, pltpu.roll(z, 1, 0))
    z_next = jnp.where(pos == GRID_W - 1, 0.0, pltpu.roll(z, tm - 1, 0))
    cw = cw_ref[...]
    zc = cw[0:1] * z_prev + cw[1:2] * z + cw[2:3] * z_next
    a = (bg * zc).astype(BF16)
    o_ref[...] += jnp.dot(a, wo_ref[...], preferred_element_type=F32)

    @pl.when(j == pl.num_programs(1) - 1)
    def _():
        o_ref[...] = x_ref[...] + mod[2:3] * o_ref[...]


def _conv(x, g, mod3, w3, cw, wo, rows_per_mod, tm):
    m, d = x.shape
    nj = d // CONV_TN
    tpm = rows_per_mod // tm
    return pl.pallas_call(
        functools.partial(_conv_kernel, tn=CONV_TN),
        grid=(m // tm, nj),
        in_specs=[
            pl.BlockSpec((tm, d), lambda i, j: (i, 0)),
            pl.BlockSpec((1, d), lambda i, j: (0, 0)),
            pl.BlockSpec((1, 3, d), lambda i, j: (i // tpm, 0, 0)),
            pl.BlockSpec((d, 3 * CONV_TN), lambda i, j: (0, j)),
            pl.BlockSpec((3, CONV_TN), lambda i, j: (0, j)),
            pl.BlockSpec((CONV_TN, d), lambda i, j: (j, 0)),
        ],
        out_specs=pl.BlockSpec((tm, d), lambda i, j: (i, 0)),
        out_shape=jax.ShapeDtypeStruct((m, d), F32),
        scratch_shapes=[pltpu.VMEM((tm, d), BF16)],
        compiler_params=_params(("parallel", "arbitrary")),
        name="conv_mixer",
    )(x, g.reshape(1, d), mod3, w3, cw, wo)


def _interleave_cols(w, parts, tn):
    d, n = w.shape[0], w.shape[1] // parts
    return w.astype(BF16).reshape(d, parts, n // tn, tn).transpose(0, 2, 1, 3).reshape(d, parts * n)


def kernel(x, c, ctx, c_ctx, w_mod, b_mod, norm_g, ffn_w_in, ffn_w_out, mlstm_w_in, mlstm_b_gate,
           mlstm_norm_g, mlstm_w_out, conv_w_in, conv_w, conv_w_out, final_norm_g):
    batch, t, d = x.shape
    tc = ctx.shape[1]
    depth = w_mod.shape[0]
    assert depth == 2 and d % (2 * HEADS * LANES) == 0 and t % GRID_W == 0
    dv = d // HEADS
    dk = dv // 2
    qk = HEADS * dk

    c_rows = jnp.concatenate([c, c_ctx[None, :], jnp.zeros((-(batch + 1) % 8, d), F32)], axis=0)
    mod = _modulation(c_rows, w_mod, b_mod).reshape(depth, c_rows.shape[0], N_MOD, d)

    def mod3(layer, sub, context=False):
        rows = mod[layer, batch:batch + 1] if context else mod[layer, :batch]
        return rows[:, 3 * sub:3 * sub + 3, :]

    tm = 512
    xf = x.reshape(batch * t, d)
    cf = ctx.reshape(batch * tc, d)

    wgu, wo = _ffn_weights(ffn_w_in[0, 0], ffn_w_out[0, 0])
    xf = _ffn(xf, norm_g[0, 0], mod3(0, 0), wgu, wo, t, tm)
    cf = _ffn(cf, norm_g[0, 0], mod3(0, 0, True), wgu, wo, batch * tc, tm)

    w_in = mlstm_w_in[0]
    w_q = w_in[:, :qk] * (dk ** -0.5)
    w_k = w_in[:, qk:2 * qk]
    w_v = w_in[:, 2 * qk:2 * qk + d]
    g0 = 2 * qk + d
    ng = 4 * HEADS
    w_gate = jnp.pad(w_in[:, g0:g0 + ng], ((0, 0), (0, LANES - ng))).astype(BF16)
    b_gate = jnp.pad(mlstm_b_gate[0], (0, LANES - ng)).reshape(1, LANES)
    w_o = w_in[:, g0 + ng:]
    w_kt = w_k.T.astype(BF16)
    w_main = jnp.concatenate([w_q, w_v, w_o], axis=1).astype(BF16)
    main, kt, gates = _proj(xf, norm_g[0, 1], mod3(0, 1), w_main, w_kt, w_gate, b_gate, t, tm)
    vc, ktc, gates_c = _proj(cf, norm_g[0, 1], mod3(0, 1, True), w_v.astype(BF16), w_kt, w_gate, b_gate,
                             tc, min(tm, tc))
    pcol, prow = _gateprep(gates, batch, SCAN_L)
    pcolc, prowc = _gateprep(gates_c, batch, SCAN_L)
    y = _scan(main, kt, pcol, prow, vc, ktc, pcolc, prowc, mlstm_norm_g[0], batch, t, tc, d)
    xf = _outproj(y.reshape(batch * t, d), xf, mod3(0, 1), mlstm_w_out[0].astype(BF16), t, tm)

    wgu, wo = _ffn_weights(ffn_w_in[0, 1], ffn_w_out[0, 1])
    xf = _ffn(xf, norm_g[0, 2], mod3(0, 2), wgu, wo, t, tm)

    wgu, wo = _ffn_weights(ffn_w_in[1, 0], ffn_w_out[1, 0])
    xf = _ffn(xf, norm_g[1, 0], mod3(1, 0), wgu, wo, t, tm)

    w3 = _interleave_cols(conv_w_in[0], 3, CONV_TN)
    xf = _conv(xf, norm_g[1, 1], mod3(1, 1), w3, conv_w[0], conv_w_out[0].astype(BF16), t, tm)

    wgu, wo = _ffn_weights(ffn_w_in[1, 1], ffn_w_out[1, 1])
    xf = _ffn(xf, norm_g[1, 2], mod3(1, 2), wgu, wo, t, tm, final_g=final_norm_g)
    return xf.reshape(batch, t, d)
```

```python
import functools

import jax
import jax.numpy as jnp
from jax import lax
from jax.experimental import pallas as pl
from jax.experimental.pallas import tpu as pltpu

F32 = jnp.float32
BF16 = jnp.bfloat16

HEADS = 8
N_MOD = 9
GRID_W = 64
RMS_EPS = 1e-6
M_INIT = -1e30

LANES = 128
MXU_COLS = 256
VMEM_LIMIT_BYTES = 60 * 1024 * 1024

NORM_ROWS = 64
EPI_ROWS = 256
ROW_TM = 1024
MIX_TM = 512
OUT_TN = 512
FFN_TF = 512
CONV_TN = 512
PROJ_TN = 1024
PROJ_TM = 1024
SCAN_L = 256
MOD_TN = 1024


def _params(sem):
    return pltpu.CompilerParams(dimension_semantics=sem, vmem_limit_bytes=VMEM_LIMIT_BYTES)


def _sigmoid(x):
    return 1.0 / (1.0 + jnp.exp(-x))


def _fill_xn(x_ref, xn_ref, g, mod):
    rows = x_ref.shape[0]
    rc = min(NORM_ROWS, rows)
    shift = mod[0:1]
    gain = g * (1.0 + mod[1:2])

    n = rows // rc

    def inv_rms(r):
        x = x_ref[pl.ds(pl.multiple_of(r * rc, rc), rc), :]
        return lax.rsqrt(jnp.mean(x * x, axis=-1, keepdims=True) + RMS_EPS)

    def body(r, inv):
        inv_next = inv_rms(jnp.minimum(r + 1, n - 1))
        sl = pl.ds(pl.multiple_of(r * rc, rc), rc)
        xn_ref[sl, :] = (x_ref[sl, :] * inv * gain + shift).astype(BF16)
        return inv_next

    lax.fori_loop(0, n, body, inv_rms(0))


def _mod_kernel(c_ref, w_ref, b_ref, o_ref):
    c = c_ref[...]
    s = (c * _sigmoid(c)).astype(BF16)
    w = w_ref[0].astype(BF16)
    o_ref[0] = jnp.dot(s, w, preferred_element_type=F32) + b_ref[0]


def _modulation(c_rows, w_mod, b_mod):
    depth, d, n = w_mod.shape
    r = c_rows.shape[0]
    return pl.pallas_call(
        _mod_kernel,
        grid=(depth, n // MOD_TN),
        in_specs=[
            pl.BlockSpec((r, d), lambda l, j: (0, 0)),
            pl.BlockSpec((1, d, MOD_TN), lambda l, j: (l, 0, j)),
            pl.BlockSpec((1, 1, MOD_TN), lambda l, j: (l, 0, j)),
        ],
        out_specs=pl.BlockSpec((1, r, MOD_TN), lambda l, j: (l, 0, j)),
        out_shape=jax.ShapeDtypeStruct((depth, r, n), F32),
        compiler_params=_params(("arbitrary", "arbitrary")),
        name="modulation",
    )(c_rows, w_mod, b_mod.reshape(depth, 1, n))


def _ffn_kernel(*refs, tf, final):
    if final:
        x_ref, g_ref, mod_ref, wg_ref, wu_ref, wo_ref, fg_ref, o_ref, xn_ref = refs
    else:
        x_ref, g_ref, mod_ref, wg_ref, wu_ref, wo_ref, o_ref, xn_ref = refs
    j = pl.program_id(1)
    mod = mod_ref[0]

    @pl.when(j == 0)
    def _():
        _fill_xn(x_ref, xn_ref, g_ref[...], mod)
        o_ref[...] = jnp.zeros_like(o_ref)

    xn = xn_ref[...]
    hg = jnp.dot(xn, wg_ref[...], preferred_element_type=F32)
    hu = jnp.dot(xn, wu_ref[...], preferred_element_type=F32)
    a = (hg * _sigmoid(hg) * hu).astype(BF16)
    d = o_ref.shape[1]
    for n0 in range(0, d, OUT_TN):
        o_ref[:, n0:n0 + OUT_TN] += jnp.dot(a, wo_ref[:, n0:n0 + OUT_TN], preferred_element_type=F32)

    @pl.when(j == pl.num_programs(1) - 1)
    def _():
        half_gate = 0.5 * mod[2:3]
        rows = x_ref.shape[0]
        rc = min(EPI_ROWS, rows)

        def body(r, carry):
            sl = pl.ds(pl.multiple_of(r * rc, rc), rc)
            y = x_ref[sl, :] + half_gate * o_ref[sl, :]
            if final:
                ms = jnp.mean(y * y, axis=-1, keepdims=True)
                y = y * lax.rsqrt(ms + RMS_EPS) * fg_ref[...]
            o_ref[sl, :] = y
            return carry

        lax.fori_loop(0, rows // rc, body, 0)


def _ffn_weights(w_in, w_out):
    f = w_in.shape[1] // 2
    fp = -(-f // FFN_TF) * FFN_TF
    wg = jnp.pad(w_in[:, :f].astype(BF16), ((0, 0), (0, fp - f)))
    wu = jnp.pad(w_in[:, f:].astype(BF16), ((0, 0), (0, fp - f)))
    wo = jnp.pad(w_out.astype(BF16), ((0, fp - f), (0, 0)))
    return wg, wu, wo


def _ffn(x, g, mod3, weights, rows_per_mod, tm, final_g=None):
    m, d = x.shape
    wg, wu, wo = weights
    fp = wo.shape[0]
    nj = fp // FFN_TF
    tpm = rows_per_mod // tm
    final = final_g is not None
    in_specs = [
        pl.BlockSpec((tm, d), lambda i, j: (i, 0)),
        pl.BlockSpec((1, d), lambda i, j: (0, 0)),
        pl.BlockSpec((1, 3, d), lambda i, j: (i // tpm, 0, 0)),
        pl.BlockSpec((d, FFN_TF), lambda i, j: (0, j)),
        pl.BlockSpec((d, FFN_TF), lambda i, j: (0, j)),
        pl.BlockSpec((FFN_TF, d), lambda i, j: (j, 0)),
    ]
    args = [x, g.reshape(1, d), mod3, wg, wu, wo]
    if final:
        in_specs.append(pl.BlockSpec((1, d), lambda i, j: (0, 0)))
        args.append(final_g.reshape(1, d))
    return pl.pallas_call(
        functools.partial(_ffn_kernel, tf=FFN_TF, final=final),
        grid=(m // tm, nj),
        in_specs=in_specs,
        out_specs=pl.BlockSpec((tm, d), lambda i, j: (i, 0)),
        out_shape=jax.ShapeDtypeStruct((m, d), F32),
        scratch_shapes=[pltpu.VMEM((tm, d), BF16)],
        compiler_params=_params(("parallel", "arbitrary")),
        name="ffn_final" if final else "ffn",
    )(*args)


def _proj_kernel(x_ref, g_ref, mod_ref, w_ref, wkt_ref, wg_ref, bg_ref,
                 main_ref, kt_ref, gates_ref, xn_ref):
    j = pl.program_id(1)

    @pl.when(j == 0)
    def _():
        _fill_xn(x_ref, xn_ref, g_ref[...], mod_ref[0])
        xn = xn_ref[...]
        kt = lax.dot_general(wkt_ref[...], xn, (((1,), (1,)), ((), ())),
                             preferred_element_type=F32)
        kt_ref[0] = kt.astype(BF16)
        gates_ref[...] = jnp.dot(xn, wg_ref[...], preferred_element_type=F32) + bg_ref[...]

    main_ref[...] = jnp.dot(xn_ref[...], w_ref[...], preferred_element_type=F32).astype(BF16)


def _proj(x, g, mod3, w_main, w_kt, w_gate, b_gate, rows_per_batch, tm):
    m, d = x.shape
    n = w_main.shape[1]
    nk = w_kt.shape[0]
    batch = m // rows_per_batch
    tpb = rows_per_batch // tm
    nmod = mod3.shape[0]
    mod_idx = (lambda i, j: (i // tpb, 0, 0)) if nmod > 1 else (lambda i, j: (0, 0, 0))
    return pl.pallas_call(
        _proj_kernel,
        grid=(m // tm, n // PROJ_TN),
        in_specs=[
            pl.BlockSpec((tm, d), lambda i, j: (i, 0)),
            pl.BlockSpec((1, d), lambda i, j: (0, 0)),
            pl.BlockSpec((1, 3, d), mod_idx),
            pl.BlockSpec((d, PROJ_TN), lambda i, j: (0, j)),
            pl.BlockSpec((nk, d), lambda i, j: (0, 0)),
            pl.BlockSpec((d, LANES), lambda i, j: (0, 0)),
            pl.BlockSpec((1, LANES), lambda i, j: (0, 0)),
        ],
        out_specs=[
            pl.BlockSpec((tm, PROJ_TN), lambda i, j: (i, j)),
            pl.BlockSpec((1, nk, tm), lambda i, j: (i // tpb, 0, i % tpb)),
            pl.BlockSpec((tm, LANES), lambda i, j: (i, 0)),
        ],
        out_shape=[
            jax.ShapeDtypeStruct((m, n), BF16),
            jax.ShapeDtypeStruct((batch, nk, rows_per_batch), BF16),
            jax.ShapeDtypeStruct((m, LANES), F32),
        ],
        scratch_shapes=[pltpu.VMEM((tm, d), BF16)],
        compiler_params=_params(("parallel", "arbitrary")),
        name="mlstm_proj",
    )(x, g.reshape(1, d), mod3, w_main, w_kt, w_gate, b_gate)


def _tri_sum(tri, x):
    x1 = x.astype(BF16)
    r1 = x - x1.astype(F32)
    x2 = r1.astype(BF16)
    x3 = (r1 - x2.astype(F32)).astype(BF16)
    dot = functools.partial(jnp.dot, preferred_element_type=F32)
    return dot(tri, x1) + dot(tri, x2) + dot(tri, x3)


def _gateprep_kernel(g_ref, pcol_ref, prow_ref, *, chunk):
    t = g_ref.shape[1]
    row = lax.broadcasted_iota(jnp.int32, (chunk, chunk), 0)
    col = lax.broadcasted_iota(jnp.int32, (chunk, chunk), 1)
    tril = jnp.where(row >= col, 1.0, 0.0).astype(BF16)
    triu = jnp.where(row <= col, 1.0, 0.0).astype(BF16)
    lane = lax.broadcasted_iota(jnp.int32, (chunk, LANES), 1)
    f_fwd = (lane >= HEADS) & (lane < 2 * HEADS)
    f_bwd = (lane >= 3 * HEADS) & (lane < 4 * HEADS)

    def body(ci, carry):
        sl = pl.ds(pl.multiple_of(ci * chunk, chunk), chunk)
        g = g_ref[0, sl, :]
        logf = jnp.minimum(g, 0.0) - jnp.log(1.0 + jnp.exp(-jnp.abs(g)))
        logf = jnp.where(f_fwd | f_bwd, logf, 0.0)
        p = jnp.where(f_fwd, _tri_sum(tril, logf), jnp.where(f_bwd, _tri_sum(triu, logf), g))
        pcol_ref[0, sl, :] = p
        prow_ref[0, :, sl] = p.T
        return carry

    lax.fori_loop(0, t // chunk, body, 0)


def _gateprep(gates, batch, chunk):
    m = gates.shape[0]
    t = m // batch
    g3 = gates.reshape(batch, t, LANES)
    return pl.pallas_call(
        functools.partial(_gateprep_kernel, chunk=chunk),
        grid=(batch,),
        in_specs=[pl.BlockSpec((1, t, LANES), lambda b: (b, 0, 0))],
        out_specs=[
            pl.BlockSpec((1, t, LANES), lambda b: (b, 0, 0)),
            pl.BlockSpec((1, LANES, t), lambda b: (b, 0, 0)),
        ],
        out_shape=[
            jax.ShapeDtypeStruct((batch, t, LANES), F32),
            jax.ShapeDtypeStruct((batch, LANES, t), F32),
        ],
        compiler_params=_params(("parallel",)),
        name="gate_prep",
    )(g3)


def _scan_kernel(q_ref, v_ref, o_ref, kt_ref, pcol_ref, prow_ref,
                 ktc_ref, vc_ref, pcolc_ref, ng_ref,
                 y_ref, hf_ref, hb_ref, *, chunk, dk, dv):
    head = pl.program_id(1)
    t = q_ref.shape[1]
    tc = vc_ref.shape[1]
    nc = t // chunk
    ncc = tc // chunk
    lane = lax.broadcasted_iota(jnp.int32, (chunk, LANES), 1)
    row = lax.broadcasted_iota(jnp.int32, (chunk, chunk), 0)
    colm = lax.broadcasted_iota(jnp.int32, (chunk, chunk), 1)
    ones = jnp.ones((chunk, LANES), BF16)
    dot = functools.partial(jnp.dot, preferred_element_type=F32)

    def column(p, idx):
        return jnp.sum(jnp.where(lane == idx, p, 0.0), axis=1, keepdims=True)

    def gate_cols(pc_ref, t0, base):
        p = pc_ref[0, pl.ds(t0, chunk), :]
        return column(p, base + head), column(p, base + HEADS + head)

    def gate_rows(pr_ref, t0, base):
        i_row = pr_ref[0, pl.ds(base + head, 1), pl.ds(t0, chunk)]
        c_row = pr_ref[0, pl.ds(base + HEADS + head, 1), pl.ds(t0, chunk)]
        return i_row, c_row

    def update(kt, vaug, i_col, cum_col, tot, caug, m):
        w_end = tot + (i_col - cum_col)
        m_new = jnp.maximum(tot + m, jnp.max(w_end, axis=0, keepdims=True))
        decay = jnp.exp(tot + m - m_new)
        wv = (jnp.exp(w_end - m_new) * vaug.astype(F32)).astype(BF16)
        return decay * caug + dot(kt, wv), m_new

    def output(q, kt, vaug, i_col, cum_col, i_row, cum_row, mask, caug, m):
        s = dot(q, kt)
        dm = jnp.where(mask, cum_col + (i_row - cum_row), -jnp.inf)
        inter = cum_col + m
        m_out = jnp.maximum(inter, jnp.max(dm, axis=1, keepdims=True))
        p = (s * jnp.exp(dm - m_out)).astype(BF16)
        r = dot(p, vaug) + jnp.exp(inter - m_out) * dot(q, caug.astype(BF16))
        inv = 1.0 / jnp.maximum(jnp.abs(r[:, dv:]), jnp.exp(-m_out))
        return jnp.concatenate([r[:, i * LANES:(i + 1) * LANES] * inv for i in range(dv // LANES)], axis=1)

    def aug(v):
        return jnp.concatenate([v, ones], axis=1)

    caug0 = jnp.zeros((dk, dv + LANES), F32)
    m0 = jnp.full((1, 1), M_INIT, F32)
    cf, mf = caug0, m0
    for ci in range(ncc):
        t0 = ci * chunk
        i_col, b_col = gate_cols(pcolc_ref, t0, 0)
        cf, mf = update(ktc_ref[0, :, pl.ds(t0, chunk)], aug(vc_ref[0, pl.ds(t0, chunk), :]),
                        i_col, b_col, b_col[chunk - 1:chunk], cf, mf)
    cb, mb = caug0, m0
    for ci in reversed(range(ncc)):
        t0 = ci * chunk
        i_col, e_col = gate_cols(pcolc_ref, t0, 2 * HEADS)
        cb, mb = update(ktc_ref[0, :, pl.ds(t0, chunk)], aug(vc_ref[0, pl.ds(t0, chunk), :]),
                        i_col, e_col, e_col[0:1], cb, mb)

    def body(ci, carry):
        cf, mf, cb, mb = carry
        t0 = pl.multiple_of(ci * chunk, chunk)
        q = q_ref[0, pl.ds(t0, chunk), :]
        kt = kt_ref[0, :, pl.ds(t0, chunk)]
        vaug = aug(v_ref[0, pl.ds(t0, chunk), :])
        i_col, b_col = gate_cols(pcol_ref, t0, 0)
        i_row, b_row = gate_rows(prow_ref, t0, 0)
        hf_ref[pl.ds(t0, chunk), :] = output(q, kt, vaug, i_col, b_col, i_row, b_row, row >= colm, cf, mf)
        cf, mf = update(kt, vaug, i_col, b_col, b_col[chunk - 1:chunk], cf, mf)
        t1 = pl.multiple_of((nc - 1 - ci) * chunk, chunk)
        q = q_ref[0, pl.ds(t1, chunk), :]
        kt = kt_ref[0, :, pl.ds(t1, chunk)]
        vaug = aug(v_ref[0, pl.ds(t1, chunk), :])
        i_col, e_col = gate_cols(pcol_ref, t1, 2 * HEADS)
        i_row, e_row = gate_rows(prow_ref, t1, 2 * HEADS)
        hb_ref[pl.ds(t1, chunk), :] = output(q, kt, vaug, i_col, e_col, i_row, e_row, row <= colm, cb, mb)
        cb, mb = update(kt, vaug, i_col, e_col, e_col[0:1], cb, mb)
        return cf, mf, cb, mb

    lax.fori_loop(0, nc, body, (cf, mf, cb, mb))

    ng = ng_ref[...]

    def finish(ci, carry):
        sl = pl.ds(pl.multiple_of(ci * chunk, chunk), chunk)
        h = hf_ref[sl, :] + hb_ref[sl, :]
        hn = h * lax.rsqrt(jnp.mean(h * h, axis=1, keepdims=True) + RMS_EPS) * ng
        y_ref[0, sl, :] = (hn * _sigmoid(o_ref[0, sl, :].astype(F32))).astype(BF16)
        return carry

    lax.fori_loop(0, nc, finish, 0)


def _scan(main, kt, pcol, prow, vc, ktc, pcolc, norm_g, batch, t, tc, d):
    dv = d // HEADS
    dk = dv // 2
    vb = HEADS * dk // dv
    ob = vb + HEADS
    main3 = main.reshape(batch, t, main.shape[1])
    vc3 = vc.reshape(batch, tc, vc.shape[1])
    return pl.pallas_call(
        functools.partial(_scan_kernel, chunk=SCAN_L, dk=dk, dv=dv),
        grid=(batch, HEADS),
        in_specs=[
            pl.BlockSpec((1, t, dk), lambda b, h: (b, 0, h)),
            pl.BlockSpec((1, t, dv), lambda b, h: (b, 0, vb + h)),
            pl.BlockSpec((1, t, dv), lambda b, h: (b, 0, ob + h)),
            pl.BlockSpec((1, dk, t), lambda b, h: (b, h, 0)),
            pl.BlockSpec((1, t, LANES), lambda b, h: (b, 0, 0)),
            pl.BlockSpec((1, LANES, t), lambda b, h: (b, 0, 0)),
            pl.BlockSpec((1, dk, tc), lambda b, h: (b, h, 0)),
            pl.BlockSpec((1, tc, dv), lambda b, h: (b, 0, h)),
            pl.BlockSpec((1, tc, LANES), lambda b, h: (b, 0, 0)),
            pl.BlockSpec((1, dv), lambda b, h: (0, h)),
        ],
        out_specs=pl.BlockSpec((1, t, dv), lambda b, h: (b, 0, h)),
        out_shape=jax.ShapeDtypeStruct((batch, t, d), BF16),
        scratch_shapes=[pltpu.VMEM((t, dv), F32), pltpu.VMEM((t, dv), F32)],
        compiler_params=_params(("parallel", "arbitrary")),
        name="mlstm_scan",
    )(main3, main3, main3, kt, pcol, prow, ktc, vc3, pcolc, norm_g.reshape(1, d))


def _outproj_kernel(y_ref, x_ref, mod_ref, w_ref, o_ref):
    gate = mod_ref[0][2:3]
    o_ref[...] = x_ref[...] + gate * jnp.dot(y_ref[...], w_ref[...], preferred_element_type=F32)


def _outproj(y, x, mod3, w, rows_per_mod, tm):
    m, d = x.shape
    tpm = rows_per_mod // tm
    return pl.pallas_call(
        _outproj_kernel,
        grid=(m // tm,),
        in_specs=[
            pl.BlockSpec((tm, d), lambda i: (i, 0)),
            pl.BlockSpec((tm, d), lambda i: (i, 0)),
            pl.BlockSpec((1, 3, d), lambda i: (i // tpm, 0, 0)),
            pl.BlockSpec((d, d), lambda i: (0, 0)),
        ],
        out_specs=pl.BlockSpec((tm, d), lambda i: (i, 0)),
        out_shape=jax.ShapeDtypeStruct((m, d), F32),
        compiler_params=_params(("parallel",)),
        name="mlstm_out",
    )(y, x, mod3, w)


def _conv_kernel(x_ref, g_ref, mod_ref, wb_ref, wc_ref, wu_ref, cw_ref, wo_ref, o_ref, xn_ref, *, tn):
    j = pl.program_id(1)
    mod = mod_ref[0]
    tm = x_ref.shape[0]

    @pl.when(j == 0)
    def _():
        _fill_xn(x_ref, xn_ref, g_ref[...], mod)
        o_ref[...] = jnp.zeros_like(o_ref)

    xn = xn_ref[...]
    bg = jnp.dot(xn, wb_ref[...], preferred_element_type=F32)
    cg = jnp.dot(xn, wc_ref[...], preferred_element_type=F32)
    u = jnp.dot(xn, wu_ref[...], preferred_element_type=F32)
    z = cg * u
    pos = lax.broadcasted_iota(jnp.int32, (tm, tn), 0) % GRID_W
    z_prev = jnp.where(pos == 0, 0.0, pltpu.roll(z, 1, 0))
    z_next = jnp.where(pos == GRID_W - 1, 0.0, pltpu.roll(z, tm - 1, 0))
    cw = cw_ref[...]
    zc = cw[0:1] * z_prev + cw[1:2] * z + cw[2:3] * z_next
    a = (bg * zc).astype(BF16)
    o_ref[...] += jnp.dot(a, wo_ref[...], preferred_element_type=F32)

    @pl.when(j == pl.num_programs(1) - 1)
    def _():
        o_ref[...] = x_ref[...] + mod[2:3] * o_ref[...]


def _conv(x, g, mod3, w3, cw, wo, rows_per_mod, tm):
    m, d = x.shape
    nj = d // CONV_TN
    tpm = rows_per_mod // tm
    return pl.pallas_call(
        functools.partial(_conv_kernel, tn=CONV_TN),
        grid=(m // tm, nj),
        in_specs=[
            pl.BlockSpec((tm, d), lambda i, j: (i, 0)),
            pl.BlockSpec((1, d), lambda i, j: (0, 0)),
            pl.BlockSpec((1, 3, d), lambda i, j: (i // tpm, 0, 0)),
            pl.BlockSpec((d, CONV_TN), lambda i, j: (0, j)),
            pl.BlockSpec((d, CONV_TN), lambda i, j: (0, nj + j)),
            pl.BlockSpec((d, CONV_TN), lambda i, j: (0, 2 * nj + j)),
            pl.BlockSpec((3, CONV_TN), lambda i, j: (0, j)),
            pl.BlockSpec((CONV_TN, d), lambda i, j: (j, 0)),
        ],
        out_specs=pl.BlockSpec((tm, d), lambda i, j: (i, 0)),
        out_shape=jax.ShapeDtypeStruct((m, d), F32),
        scratch_shapes=[pltpu.VMEM((tm, d), BF16)],
        compiler_params=_params(("parallel", "arbitrary")),
        name="conv_mixer",
    )(x, g.reshape(1, d), mod3, w3, w3, w3, cw, wo)


def kernel(x, c, ctx, c_ctx, w_mod, b_mod, norm_g, ffn_w_in, ffn_w_out, mlstm_w_in, mlstm_b_gate,
           mlstm_norm_g, mlstm_w_out, conv_w_in, conv_w, conv_w_out, final_norm_g):
    batch, t, d = x.shape
    tc = ctx.shape[1]
    depth = w_mod.shape[0]
    assert depth == 2 and d % (2 * HEADS * LANES) == 0 and t % GRID_W == 0
    dv = d // HEADS
    dk = dv // 2
    qk = HEADS * dk

    c_rows = jnp.concatenate([c, c_ctx[None, :], jnp.zeros((-(batch + 1) % 8, d), F32)], axis=0)
    mod = _modulation(c_rows, w_mod, b_mod).reshape(depth, c_rows.shape[0], N_MOD, d)

    def mod3(layer, sub, context=False):
        rows = mod[layer, batch:batch + 1] if context else mod[layer, :batch]
        return rows[:, 3 * sub:3 * sub + 3, :]

    tm = min(ROW_TM, t)
    tmix = min(MIX_TM, t)
    xf = x.reshape(batch * t, d)
    cf = ctx.reshape(batch * tc, d)

    ffn_w = _ffn_weights(ffn_w_in[0, 0], ffn_w_out[0, 0])
    xf = _ffn(xf, norm_g[0, 0], mod3(0, 0), ffn_w, t, tm)
    cf = _ffn(cf, norm_g[0, 0], mod3(0, 0, True), ffn_w, batch * tc, tm)

    w_in = mlstm_w_in[0]
    w_q = (w_in[:, :qk] * (dk ** -0.5)).astype(BF16)
    w_kt = w_in[:, qk:2 * qk].astype(BF16).T
    w_v = w_in[:, 2 * qk:2 * qk + d].astype(BF16)
    g0 = 2 * qk + d
    ng = 4 * HEADS
    w_gate = jnp.pad(w_in[:, g0:g0 + ng].astype(BF16), ((0, 0), (0, LANES - ng)))
    b_gate = jnp.pad(mlstm_b_gate[0], (0, LANES - ng)).reshape(1, LANES)
    w_o = w_in[:, g0 + ng:].astype(BF16)
    w_main = jnp.concatenate([w_q, w_v, w_o], axis=1)
    main, kt, gates = _proj(xf, norm_g[0, 1], mod3(0, 1), w_main, w_kt, w_gate, b_gate, t,
                            min(PROJ_TM, t))
    vc, ktc, gates_c = _proj(cf, norm_g[0, 1], mod3(0, 1, True), w_v, w_kt, w_gate, b_gate,
                             tc, min(PROJ_TM, tc))
    pcol, prow = _gateprep(gates, batch, SCAN_L)
    pcolc, _ = _gateprep(gates_c, batch, SCAN_L)
    y = _scan(main, kt, pcol, prow, vc, ktc, pcolc, mlstm_norm_g[0], batch, t, tc, d)
    xf = _outproj(y.reshape(batch * t, d), xf, mod3(0, 1), mlstm_w_out[0].astype(BF16), t, tmix)

    xf = _ffn(xf, norm_g[0, 2], mod3(0, 2), _ffn_weights(ffn_w_in[0, 1], ffn_w_out[0, 1]), t, tm)

    xf = _ffn(xf, norm_g[1, 0], mod3(1, 0), _ffn_weights(ffn_w_in[1, 0], ffn_w_out[1, 0]), t, tm)
    xf = _conv(xf, norm_g[1, 1], mod3(1, 1), conv_w_in[0].astype(BF16), conv_w[0],
               conv_w_out[0].astype(BF16), t, tmix)
    xf = _ffn(xf, norm_g[1, 2], mod3(1, 2), _ffn_weights(ffn_w_in[1, 1], ffn_w_out[1, 1]), t, tm,
              final_g=final_norm_g)
    return xf.reshape(batch, t, d)
```

```python
import functools

import jax
import jax.numpy as jnp
from jax import lax
from jax.experimental import pallas as pl
from jax.experimental.pallas import tpu as pltpu

F32 = jnp.float32
BF16 = jnp.bfloat16

HEADS = 8
N_MOD = 9
GRID_W = 64
RMS_EPS = 1e-6
M_INIT = -1e30

LANES = 128
MXU_COLS = 256
VMEM_LIMIT_BYTES = 60 * 1024 * 1024

NORM_ROWS = 64
EPI_ROWS = 256
ROW_TM = 1024
MIX_TM = 512
OUT_TN = 512
FFN_TF = 512
CONV_TN = 512
PROJ_TN = 1024
PROJ_TM = 1024
SCAN_L = 256
MOD_TN = 1024


def _params(sem):
    return pltpu.CompilerParams(dimension_semantics=sem, vmem_limit_bytes=VMEM_LIMIT_BYTES)


def _sigmoid(x):
    return 1.0 / (1.0 + jnp.exp(-x))


def _fill_xn(x_ref, xn_ref, g, mod):
    rows = x_ref.shape[0]
    rc = min(NORM_ROWS, rows)
    shift = mod[0:1]
    gain = g * (1.0 + mod[1:2])

    n = rows // rc

    def inv_rms(r):
        x = x_ref[pl.ds(pl.multiple_of(r * rc, rc), rc), :]
        return lax.rsqrt(jnp.mean(x * x, axis=-1, keepdims=True) + RMS_EPS)

    def body(r, inv):
        inv_next = inv_rms(jnp.minimum(r + 1, n - 1))
        sl = pl.ds(pl.multiple_of(r * rc, rc), rc)
        xn_ref[sl, :] = (x_ref[sl, :] * inv * gain + shift).astype(BF16)
        return inv_next

    lax.fori_loop(0, n, body, inv_rms(0))


def _mod_kernel(c_ref, w_ref, b_ref, o_ref):
    c = c_ref[...]
    s = (c * _sigmoid(c)).astype(BF16)
    w = w_ref[0].astype(BF16)
    o_ref[0] = jnp.dot(s, w, preferred_element_type=F32) + b_ref[0]


def _modulation(c_rows, w_mod, b_mod):
    depth, d, n = w_mod.shape
    r = c_rows.shape[0]
    return pl.pallas_call(
        _mod_kernel,
        grid=(depth, n // MOD_TN),
        in_specs=[
            pl.BlockSpec((r, d), lambda l, j: (0, 0)),
            pl.BlockSpec((1, d, MOD_TN), lambda l, j: (l, 0, j)),
            pl.BlockSpec((1, 1, MOD_TN), lambda l, j: (l, 0, j)),
        ],
        out_specs=pl.BlockSpec((1, r, MOD_TN), lambda l, j: (l, 0, j)),
        out_shape=jax.ShapeDtypeStruct((depth, r, n), F32),
        compiler_params=_params(("arbitrary", "arbitrary")),
        name="modulation",
    )(c_rows, w_mod, b_mod.reshape(depth, 1, n))


def _ffn_tile_start(j, f):
    assert f % LANES == 0 and f >= FFN_TF
    return pl.multiple_of(jnp.minimum(j * FFN_TF, f - FFN_TF), LANES)


def _ffn_kernel(*refs, f, final):
    if final:
        x_ref, g_ref, mod_ref, wg_ref, wu_ref, wo_ref, fg_ref, o_ref, xn_ref = refs
    else:
        x_ref, g_ref, mod_ref, wg_ref, wu_ref, wo_ref, o_ref, xn_ref = refs
    j = pl.program_id(1)
    mod = mod_ref[0]

    @pl.when(j == 0)
    def _():
        _fill_xn(x_ref, xn_ref, g_ref[...], mod)
        o_ref[...] = jnp.zeros_like(o_ref)

    xn = xn_ref[...]
    hg = jnp.dot(xn, wg_ref[0, 0], preferred_element_type=F32)
    hu = jnp.dot(xn, wu_ref[0, 0], preferred_element_type=F32)
    done = j * FFN_TF - _ffn_tile_start(j, f)
    col = lax.broadcasted_iota(jnp.int32, hu.shape, 1)
    a = (hg * _sigmoid(hg) * jnp.where(col >= done, hu, 0.0)).astype(BF16)
    d = o_ref.shape[1]
    for n0 in range(0, d, OUT_TN):
        o_ref[:, n0:n0 + OUT_TN] += jnp.dot(a, wo_ref[0, 0, :, n0:n0 + OUT_TN], preferred_element_type=F32)

    @pl.when(j == pl.num_programs(1) - 1)
    def _():
        half_gate = 0.5 * mod[2:3]
        rows = x_ref.shape[0]
        rc = min(EPI_ROWS, rows)

        def body(r, carry):
            sl = pl.ds(pl.multiple_of(r * rc, rc), rc)
            y = x_ref[sl, :] + half_gate * o_ref[sl, :]
            if final:
                ms = jnp.mean(y * y, axis=-1, keepdims=True)
                y = y * lax.rsqrt(ms + RMS_EPS) * fg_ref[...]
            o_ref[sl, :] = y
            return carry

        lax.fori_loop(0, rows // rc, body, 0)


def _ffn(x, g, mod3, w_in, w_out, layer, sub, rows_per_mod, tm, final_g=None):
    m, d = x.shape
    f = w_out.shape[2]
    nj = -(-f // FFN_TF)
    tpm = rows_per_mod // tm
    final = final_g is not None
    in_specs = [
        pl.BlockSpec((tm, d), lambda i, j: (i, 0)),
        pl.BlockSpec((1, d), lambda i, j: (0, 0)),
        pl.BlockSpec((1, 3, d), lambda i, j: (i // tpm, 0, 0)),
        pl.BlockSpec((pl.Element(1), pl.Element(1), pl.Element(d), pl.Element(FFN_TF)),
                     lambda i, j: (layer, sub, 0, _ffn_tile_start(j, f))),
        pl.BlockSpec((pl.Element(1), pl.Element(1), pl.Element(d), pl.Element(FFN_TF)),
                     lambda i, j: (layer, sub, 0, pl.multiple_of(f + _ffn_tile_start(j, f), LANES))),
        pl.BlockSpec((pl.Element(1), pl.Element(1), pl.Element(FFN_TF), pl.Element(d)),
                     lambda i, j: (layer, sub, _ffn_tile_start(j, f), 0)),
    ]
    args = [x, g.reshape(1, d), mod3, w_in, w_in, w_out]
    if final:
        in_specs.append(pl.BlockSpec((1, d), lambda i, j: (0, 0)))
        args.append(final_g.reshape(1, d))
    return pl.pallas_call(
        functools.partial(_ffn_kernel, f=f, final=final),
        grid=(m // tm, nj),
        in_specs=in_specs,
        out_specs=pl.BlockSpec((tm, d), lambda i, j: (i, 0)),
        out_shape=jax.ShapeDtypeStruct((m, d), F32),
        scratch_shapes=[pltpu.VMEM((tm, d), BF16)],
        compiler_params=_params(("parallel", "arbitrary")),
        name="ffn_final" if final else "ffn",
    )(*args)


def _proj_kernel(x_ref, g_ref, mod_ref, w_ref, wkt_ref, wg_ref, bg_ref,
                 main_ref, kt_ref, gates_ref, xn_ref):
    j = pl.program_id(1)

    @pl.when(j == 0)
    def _():
        _fill_xn(x_ref, xn_ref, g_ref[...], mod_ref[0])
        xn = xn_ref[...]
        kt = lax.dot_general(wkt_ref[...], xn, (((1,), (1,)), ((), ())),
                             preferred_element_type=F32)
        kt_ref[0] = kt.astype(BF16)
        gates_ref[...] = jnp.dot(xn, wg_ref[...], preferred_element_type=F32) + bg_ref[...]

    main_ref[...] = jnp.dot(xn_ref[...], w_ref[...], preferred_element_type=F32).astype(BF16)


def _proj(x, g, mod3, w_main, w_kt, w_gate, b_gate, rows_per_batch, tm):
    m, d = x.shape
    n = w_main.shape[1]
    nk = w_kt.shape[0]
    batch = m // rows_per_batch
    tpb = rows_per_batch // tm
    nmod = mod3.shape[0]
    mod_idx = (lambda i, j: (i // tpb, 0, 0)) if nmod > 1 else (lambda i, j: (0, 0, 0))
    return pl.pallas_call(
        _proj_kernel,
        grid=(m // tm, n // PROJ_TN),
        in_specs=[
            pl.BlockSpec((tm, d), lambda i, j: (i, 0)),
            pl.BlockSpec((1, d), lambda i, j: (0, 0)),
            pl.BlockSpec((1, 3, d), mod_idx),
            pl.BlockSpec((d, PROJ_TN), lambda i, j: (0, j)),
            pl.BlockSpec((nk, d), lambda i, j: (0, 0)),
            pl.BlockSpec((d, LANES), lambda i, j: (0, 0)),
            pl.BlockSpec((1, LANES), lambda i, j: (0, 0)),
        ],
        out_specs=[
            pl.BlockSpec((tm, PROJ_TN), lambda i, j: (i, j)),
            pl.BlockSpec((1, nk, tm), lambda i, j: (i // tpb, 0, i % tpb)),
            pl.BlockSpec((tm, LANES), lambda i, j: (i, 0)),
        ],
        out_shape=[
            jax.ShapeDtypeStruct((m, n), BF16),
            jax.ShapeDtypeStruct((batch, nk, rows_per_batch), BF16),
            jax.ShapeDtypeStruct((m, LANES), F32),
        ],
        scratch_shapes=[pltpu.VMEM((tm, d), BF16)],
        compiler_params=_params(("parallel", "arbitrary")),
        name="mlstm_proj",
    )(x, g.reshape(1, d), mod3, w_main, w_kt, w_gate, b_gate)


def _tri_sum(tri, x):
    x1 = x.astype(BF16)
    r1 = x - x1.astype(F32)
    x2 = r1.astype(BF16)
    x3 = (r1 - x2.astype(F32)).astype(BF16)
    dot = functools.partial(jnp.dot, preferred_element_type=F32)
    return dot(tri, x1) + dot(tri, x2) + dot(tri, x3)


def _gateprep_kernel(g_ref, pcol_ref, prow_ref, *, chunk):
    t = g_ref.shape[1]
    row = lax.broadcasted_iota(jnp.int32, (chunk, chunk), 0)
    col = lax.broadcasted_iota(jnp.int32, (chunk, chunk), 1)
    tril = jnp.where(row >= col, 1.0, 0.0).astype(BF16)
    triu = jnp.where(row <= col, 1.0, 0.0).astype(BF16)
    lane = lax.broadcasted_iota(jnp.int32, (chunk, LANES), 1)
    f_fwd = (lane >= HEADS) & (lane < 2 * HEADS)
    f_bwd = (lane >= 3 * HEADS) & (lane < 4 * HEADS)

    def body(ci, carry):
        sl = pl.ds(pl.multiple_of(ci * chunk, chunk), chunk)
        g = g_ref[0, sl, :]
        logf = jnp.minimum(g, 0.0) - jnp.log(1.0 + jnp.exp(-jnp.abs(g)))
        logf = jnp.where(f_fwd | f_bwd, logf, 0.0)
        p = jnp.where(f_fwd, _tri_sum(tril, logf), jnp.where(f_bwd, _tri_sum(triu, logf), g))
        pcol_ref[0, sl, :] = p
        prow_ref[0, :, sl] = p.T
        return carry

    lax.fori_loop(0, t // chunk, body, 0)


def _gateprep(gates, batch, chunk):
    m = gates.shape[0]
    t = m // batch
    g3 = gates.reshape(batch, t, LANES)
    return pl.pallas_call(
        functools.partial(_gateprep_kernel, chunk=chunk),
        grid=(batch,),
        in_specs=[pl.BlockSpec((1, t, LANES), lambda b: (b, 0, 0))],
        out_specs=[
            pl.BlockSpec((1, t, LANES), lambda b: (b, 0, 0)),
            pl.BlockSpec((1, LANES, t), lambda b: (b, 0, 0)),
        ],
        out_shape=[
            jax.ShapeDtypeStruct((batch, t, LANES), F32),
            jax.ShapeDtypeStruct((batch, LANES, t), F32),
        ],
        compiler_params=_params(("parallel",)),
        name="gate_prep",
    )(g3)


def _scan_kernel(q_ref, v_ref, o_ref, kt_ref, pcol_ref, prow_ref,
                 ktc_ref, vc_ref, pcolc_ref, ng_ref,
                 y_ref, hf_ref, hb_ref, *, chunk, dk, dv):
    head = pl.program_id(1)
    t = q_ref.shape[1]
    tc = vc_ref.shape[1]
    nc = t // chunk
    ncc = tc // chunk
    lane = lax.broadcasted_iota(jnp.int32, (chunk, LANES), 1)
    row = lax.broadcasted_iota(jnp.int32, (chunk, chunk), 0)
    colm = lax.broadcasted_iota(jnp.int32, (chunk, chunk), 1)
    ones = jnp.ones((chunk, LANES), BF16)
    dot = functools.partial(jnp.dot, preferred_element_type=F32)

    def column(p, idx):
        return jnp.sum(jnp.where(lane == idx, p, 0.0), axis=1, keepdims=True)

    def gate_cols(pc_ref, t0, base):
        p = pc_ref[0, pl.ds(t0, chunk), :]
        return column(p, base + head), column(p, base + HEADS + head)

    def gate_rows(pr_ref, t0, base):
        i_row = pr_ref[0, pl.ds(base + head, 1), pl.ds(t0, chunk)]
        c_row = pr_ref[0, pl.ds(base + HEADS + head, 1), pl.ds(t0, chunk)]
        return i_row, c_row

    def update(kt, vaug, i_col, cum_col, tot, caug, m):
        w_end = tot + (i_col - cum_col)
        m_new = jnp.maximum(tot + m, jnp.max(w_end, axis=0, keepdims=True))
        decay = jnp.exp(tot + m - m_new)
        wv = (jnp.exp(w_end - m_new) * vaug.astype(F32)).astype(BF16)
        return decay * caug + dot(kt, wv), m_new

    def output(q, kt, vaug, i_col, cum_col, i_row, cum_row, mask, caug, m):
        s = dot(q, kt)
        dm = jnp.where(mask, cum_col + (i_row - cum_row), -jnp.inf)
        inter = cum_col + m
        m_out = jnp.maximum(inter, jnp.max(dm, axis=1, keepdims=True))
        p = (s * jnp.exp(dm - m_out)).astype(BF16)
        r = dot(p, vaug) + jnp.exp(inter - m_out) * dot(q, caug.astype(BF16))
        inv = 1.0 / jnp.maximum(jnp.abs(r[:, dv:]), jnp.exp(-m_out))
        return jnp.concatenate([r[:, i * LANES:(i + 1) * LANES] * inv for i in range(dv // LANES)], axis=1)

    def aug(v):
        return jnp.concatenate([v, ones], axis=1)

    caug0 = jnp.zeros((dk, dv + LANES), F32)
    m0 = jnp.full((1, 1), M_INIT, F32)
    cf, mf = caug0, m0
    for ci in range(ncc):
        t0 = ci * chunk
        i_col, b_col = gate_cols(pcolc_ref, t0, 0)
        cf, mf = update(ktc_ref[0, :, pl.ds(t0, chunk)], aug(vc_ref[0, pl.ds(t0, chunk), :]),
                        i_col, b_col, b_col[chunk - 1:chunk], cf, mf)
    cb, mb = caug0, m0
    for ci in reversed(range(ncc)):
        t0 = ci * chunk
        i_col, e_col = gate_cols(pcolc_ref, t0, 2 * HEADS)
        cb, mb = update(ktc_ref[0, :, pl.ds(t0, chunk)], aug(vc_ref[0, pl.ds(t0, chunk), :]),
                        i_col, e_col, e_col[0:1], cb, mb)

    def body(ci, carry):
        cf, mf, cb, mb = carry
        t0 = pl.multiple_of(ci * chunk, chunk)
        q = q_ref[0, pl.ds(t0, chunk), :]
        kt = kt_ref[0, :, pl.ds(t0, chunk)]
        vaug = aug(v_ref[0, pl.ds(t0, chunk), :])
        i_col, b_col = gate_cols(pcol_ref, t0, 0)
        i_row, b_row = gate_rows(prow_ref, t0, 0)
        hf_ref[pl.ds(t0, chunk), :] = output(q, kt, vaug, i_col, b_col, i_row, b_row, row >= colm, cf, mf)
        cf, mf = update(kt, vaug, i_col, b_col, b_col[chunk - 1:chunk], cf, mf)
        t1 = pl.multiple_of((nc - 1 - ci) * chunk, chunk)
        q = q_ref[0, pl.ds(t1, chunk), :]
        kt = kt_ref[0, :, pl.ds(t1, chunk)]
        vaug = aug(v_ref[0, pl.ds(t1, chunk), :])
        i_col, e_col = gate_cols(pcol_ref, t1, 2 * HEADS)
        i_row, e_row = gate_rows(prow_ref, t1, 2 * HEADS)
        hb_ref[pl.ds(t1, chunk), :] = output(q, kt, vaug, i_col, e_col, i_row, e_row, row <= colm, cb, mb)
        cb, mb = update(kt, vaug, i_col, e_col, e_col[0:1], cb, mb)
        return cf, mf, cb, mb

    lax.fori_loop(0, nc, body, (cf, mf, cb, mb))

    ng = ng_ref[...]

    def finish(ci, carry):
        sl = pl.ds(pl.multiple_of(ci * chunk, chunk), chunk)
        h = hf_ref[sl, :] + hb_ref[sl, :]
        hn = h * lax.rsqrt(jnp.mean(h * h, axis=1, keepdims=True) + RMS_EPS) * ng
        y_ref[0, sl, :] = (hn * _sigmoid(o_ref[0, sl, :].astype(F32))).astype(BF16)
        return carry

    lax.fori_loop(0, nc, finish, 0)


def _scan(main, kt, pcol, prow, vc, ktc, pcolc, norm_g, batch, t, tc, d):
    dv = d // HEADS
    dk = dv // 2
    vb = HEADS * dk // dv
    ob = vb + HEADS
    main3 = main.reshape(batch, t, main.shape[1])
    vc3 = vc.reshape(batch, tc, vc.shape[1])
    return pl.pallas_call(
        functools.partial(_scan_kernel, chunk=SCAN_L, dk=dk, dv=dv),
        grid=(batch, HEADS),
        in_specs=[
            pl.BlockSpec((1, t, dk), lambda b, h: (b, 0, h)),
            pl.BlockSpec((1, t, dv), lambda b, h: (b, 0, vb + h)),
            pl.BlockSpec((1, t, dv), lambda b, h: (b, 0, ob + h)),
            pl.BlockSpec((1, dk, t), lambda b, h: (b, h, 0)),
            pl.BlockSpec((1, t, LANES), lambda b, h: (b, 0, 0)),
            pl.BlockSpec((1, LANES, t), lambda b, h: (b, 0, 0)),
            pl.BlockSpec((1, dk, tc), lambda b, h: (b, h, 0)),
            pl.BlockSpec((1, tc, dv), lambda b, h: (b, 0, h)),
            pl.BlockSpec((1, tc, LANES), lambda b, h: (b, 0, 0)),
            pl.BlockSpec((1, dv), lambda b, h: (0, h)),
        ],
        out_specs=pl.BlockSpec((1, t, dv), lambda b, h: (b, 0, h)),
        out_shape=jax.ShapeDtypeStruct((batch, t, d), BF16),
        scratch_shapes=[pltpu.VMEM((t, dv), F32), pltpu.VMEM((t, dv), F32)],
        compiler_params=_params(("parallel", "arbitrary")),
        name="mlstm_scan",
    )(main3, main3, main3, kt, pcol, prow, ktc, vc3, pcolc, norm_g.reshape(1, d))


def _outproj_kernel(y_ref, x_ref, mod_ref, w_ref, o_ref):
    gate = mod_ref[0][2:3]
    o_ref[...] = x_ref[...] + gate * jnp.dot(y_ref[...], w_ref[...], preferred_element_type=F32)


def _outproj(y, x, mod3, w, rows_per_mod, tm):
    m, d = x.shape
    tpm = rows_per_mod // tm
    return pl.pallas_call(
        _outproj_kernel,
        grid=(m // tm,),
        in_specs=[
            pl.BlockSpec((tm, d), lambda i: (i, 0)),
            pl.BlockSpec((tm, d), lambda i: (i, 0)),
            pl.BlockSpec((1, 3, d), lambda i: (i // tpm, 0, 0)),
            pl.BlockSpec((d, d), lambda i: (0, 0)),
        ],
        out_specs=pl.BlockSpec((tm, d), lambda i: (i, 0)),
        out_shape=jax.ShapeDtypeStruct((m, d), F32),
        compiler_params=_params(("parallel",)),
        name="mlstm_out",
    )(y, x, mod3, w)


def _conv_kernel(x_ref, g_ref, mod_ref, wb_ref, wc_ref, wu_ref, cw_ref, wo_ref, o_ref, xn_ref, *, tn):
    j = pl.program_id(1)
    mod = mod_ref[0]
    tm = x_ref.shape[0]

    @pl.when(j == 0)
    def _():
        _fill_xn(x_ref, xn_ref, g_ref[...], mod)
        o_ref[...] = jnp.zeros_like(o_ref)

    xn = xn_ref[...]
    bg = jnp.dot(xn, wb_ref[...], preferred_element_type=F32)
    cg = jnp.dot(xn, wc_ref[...], preferred_element_type=F32)
    u = jnp.dot(xn, wu_ref[...], preferred_element_type=F32)
    z = cg * u
    pos = lax.broadcasted_iota(jnp.int32, (tm, tn), 0) % GRID_W
    z_prev = jnp.where(pos == 0, 0.0, pltpu.roll(z, 1, 0))
    z_next = jnp.where(pos == GRID_W - 1, 0.0, pltpu.roll(z, tm - 1, 0))
    cw = cw_ref[...]
    zc = cw[0:1] * z_prev + cw[1:2] * z + cw[2:3] * z_next
    a = (bg * zc).astype(BF16)
    o_ref[...] += jnp.dot(a, wo_ref[...], preferred_element_type=F32)

    @pl.when(j == pl.num_programs(1) - 1)
    def _():
        o_ref[...] = x_ref[...] + mod[2:3] * o_ref[...]


def _conv(x, g, mod3, w3, cw, wo, rows_per_mod, tm):
    m, d = x.shape
    nj = d // CONV_TN
    tpm = rows_per_mod // tm
    return pl.pallas_call(
        functools.partial(_conv_kernel, tn=CONV_TN),
        grid=(m // tm, nj),
        in_specs=[
            pl.BlockSpec((tm, d), lambda i, j: (i, 0)),
            pl.BlockSpec((1, d), lambda i, j: (0, 0)),
            pl.BlockSpec((1, 3, d), lambda i, j: (i // tpm, 0, 0)),
            pl.BlockSpec((d, CONV_TN), lambda i, j: (0, j)),
            pl.BlockSpec((d, CONV_TN), lambda i, j: (0, nj + j)),
            pl.BlockSpec((d, CONV_TN), lambda i, j: (0, 2 * nj + j)),
            pl.BlockSpec((3, CONV_TN), lambda i, j: (0, j)),
            pl.BlockSpec((CONV_TN, d), lambda i, j: (j, 0)),
        ],
        out_specs=pl.BlockSpec((tm, d), lambda i, j: (i, 0)),
        out_shape=jax.ShapeDtypeStruct((m, d), F32),
        scratch_shapes=[pltpu.VMEM((tm, d), BF16)],
        compiler_params=_params(("parallel", "arbitrary")),
        name="conv_mixer",
    )(x, g.reshape(1, d), mod3, w3, w3, w3, cw, wo)


def kernel(x, c, ctx, c_ctx, w_mod, b_mod, norm_g, ffn_w_in, ffn_w_out, mlstm_w_in, mlstm_b_gate,
           mlstm_norm_g, mlstm_w_out, conv_w_in, conv_w, conv_w_out, final_norm_g):
    batch, t, d = x.shape
    tc = ctx.shape[1]
    depth = w_mod.shape[0]
    assert depth == 2 and d % (2 * HEADS * LANES) == 0 and t % GRID_W == 0
    dv = d // HEADS
    dk = dv // 2
    qk = HEADS * dk

    c_rows = jnp.concatenate([c, c_ctx[None, :], jnp.zeros((-(batch + 1) % 8, d), F32)], axis=0)
    mod = _modulation(c_rows, w_mod, b_mod).reshape(depth, c_rows.shape[0], N_MOD, d)

    def mod3(layer, sub, context=False):
        rows = mod[layer, batch:batch + 1] if context else mod[layer, :batch]
        return rows[:, 3 * sub:3 * sub + 3, :]

    tm = min(ROW_TM, t)
    tmix = min(MIX_TM, t)
    xf = x.reshape(batch * t, d)
    cf = ctx.reshape(batch * tc, d)

    ffn_in = ffn_w_in.astype(BF16)
    ffn_out = ffn_w_out.astype(BF16)
    xf = _ffn(xf, norm_g[0, 0], mod3(0, 0), ffn_in, ffn_out, 0, 0, t, tm)
    cf = _ffn(cf, norm_g[0, 0], mod3(0, 0, True), ffn_in, ffn_out, 0, 0, batch * tc, tm)

    w_in = mlstm_w_in[0]
    w_q = (w_in[:, :qk] * (dk ** -0.5)).astype(BF16)
    w_kt = w_in[:, qk:2 * qk].astype(BF16).T
    w_v = w_in[:, 2 * qk:2 * qk + d].astype(BF16)
    g0 = 2 * qk + d
    ng = 4 * HEADS
    w_gate = jnp.pad(w_in[:, g0:g0 + ng].astype(BF16), ((0, 0), (0, LANES - ng)))
    b_gate = jnp.pad(mlstm_b_gate[0], (0, LANES - ng)).reshape(1, LANES)
    w_o = w_in[:, g0 + ng:].astype(BF16)
    w_main = jnp.concatenate([w_q, w_v, w_o], axis=1)
    main, kt, gates = _proj(xf, norm_g[0, 1], mod3(0, 1), w_main, w_kt, w_gate, b_gate, t,
                            min(PROJ_TM, t))
    vc, ktc, gates_c = _proj(cf, norm_g[0, 1], mod3(0, 1, True), w_v, w_kt, w_gate, b_gate,
                             tc, min(PROJ_TM, tc))
    pcol, prow = _gateprep(gates, batch, SCAN_L)
    pcolc, _ = _gateprep(gates_c, batch, SCAN_L)
    y = _scan(main, kt, pcol, prow, vc, ktc, pcolc, mlstm_norm_g[0], batch, t, tc, d)
    xf = _outproj(y.reshape(batch * t, d), xf, mod3(0, 1), mlstm_w_out[0].astype(BF16), t, tmix)

    xf = _ffn(xf, norm_g[0, 2], mod3(0, 2), ffn_in, ffn_out, 0, 1, t, tm)

    xf = _ffn(xf, norm_g[1, 0], mod3(1, 0), ffn_in, ffn_out, 1, 0, t, tm)
    xf = _conv(xf, norm_g[1, 1], mod3(1, 1), conv_w_in[0].astype(BF16), conv_w[0],
               conv_w_out[0].astype(BF16), t, tmix)
    xf = _ffn(xf, norm_g[1, 2], mod3(1, 2), ffn_in, ffn_out, 1, 1, t, tm,
              final_g=final_norm_g)
    return xf.reshape(batch, t, d)
```

```python
import functools

import jax
import jax.numpy as jnp
from jax import lax
from jax.experimental import pallas as pl
from jax.experimental.pallas import tpu as pltpu

F32 = jnp.float32
BF16 = jnp.bfloat16

HEADS = 8
N_MOD = 9
GRID_W = 64
RMS_EPS = 1e-6
M_INIT = -1e30

LANES = 128
MXU_COLS = 256
VMEM_LIMIT_BYTES = 60 * 1024 * 1024

NORM_ROWS = 64
EPI_ROWS = 256
ROW_TM = 1024
MIX_TM = 512
OUT_TN = 512
FFN_TF = 512
CONV_TN = 512
PROJ_TN = 1024
PROJ_TM = 1024
SCAN_L = 256
RUNMAX_LANE = 4 * HEADS
SCAN_UNROLL_A = 4
SCAN_UNROLL_C = 4
MOD_TN = 1024


def _params(sem):
    return pltpu.CompilerParams(dimension_semantics=sem, vmem_limit_bytes=VMEM_LIMIT_BYTES)


def _sigmoid(x):
    return 1.0 / (1.0 + jnp.exp(-x))


def _fill_xn(x_ref, xn_ref, g, mod):
    rows = x_ref.shape[0]
    rc = min(NORM_ROWS, rows)
    shift = mod[0:1]
    gain = g * (1.0 + mod[1:2])

    n = rows // rc

    def inv_rms(r):
        x = x_ref[pl.ds(pl.multiple_of(r * rc, rc), rc), :]
        return lax.rsqrt(jnp.mean(x * x, axis=-1, keepdims=True) + RMS_EPS)

    def body(r, inv):
        inv_next = inv_rms(jnp.minimum(r + 1, n - 1))
        sl = pl.ds(pl.multiple_of(r * rc, rc), rc)
        xn_ref[sl, :] = (x_ref[sl, :] * inv * gain + shift).astype(BF16)
        return inv_next

    lax.fori_loop(0, n, body, inv_rms(0))


def _mod_kernel(c_ref, w_ref, b_ref, o_ref):
    c = c_ref[...]
    s = (c * _sigmoid(c)).astype(BF16)
    w = w_ref[0].astype(BF16)
    o_ref[0] = jnp.dot(s, w, preferred_element_type=F32) + b_ref[0]


def _modulation(c_rows, w_mod, b_mod):
    depth, d, n = w_mod.shape
    r = c_rows.shape[0]
    return pl.pallas_call(
        _mod_kernel,
        grid=(depth, n // MOD_TN),
        in_specs=[
            pl.BlockSpec((r, d), lambda l, j: (0, 0)),
            pl.BlockSpec((1, d, MOD_TN), lambda l, j: (l, 0, j)),
            pl.BlockSpec((1, 1, MOD_TN), lambda l, j: (l, 0, j)),
        ],
        out_specs=pl.BlockSpec((1, r, MOD_TN), lambda l, j: (l, 0, j)),
        out_shape=jax.ShapeDtypeStruct((depth, r, n), F32),
        compiler_params=_params(("arbitrary", "arbitrary")),
        name="modulation",
    )(c_rows, w_mod, b_mod.reshape(depth, 1, n))


def _ffn_tile_start(j, f):
    assert f % LANES == 0 and f >= FFN_TF
    return pl.multiple_of(jnp.minimum(j * FFN_TF, f - FFN_TF), LANES)


def _ffn_kernel(*refs, f, final):
    if final:
        x_ref, g_ref, mod_ref, wg_ref, wu_ref, wo_ref, fg_ref, o_ref, xn_ref = refs
    else:
        x_ref, g_ref, mod_ref, wg_ref, wu_ref, wo_ref, o_ref, xn_ref = refs
    j = pl.program_id(1)
    mod = mod_ref[0]

    @pl.when(j == 0)
    def _():
        _fill_xn(x_ref, xn_ref, g_ref[...], mod)
        o_ref[...] = jnp.zeros_like(o_ref)

    xn = xn_ref[...]
    hg = jnp.dot(xn, wg_ref[0, 0], preferred_element_type=F32)
    hu = jnp.dot(xn, wu_ref[0, 0], preferred_element_type=F32)
    done = j * FFN_TF - _ffn_tile_start(j, f)
    col = lax.broadcasted_iota(jnp.int32, hu.shape, 1)
    a = (hg * _sigmoid(hg) * jnp.where(col >= done, hu, 0.0)).astype(BF16)
    d = o_ref.shape[1]
    for n0 in range(0, d, OUT_TN):
        o_ref[:, n0:n0 + OUT_TN] += jnp.dot(a, wo_ref[0, 0, :, n0:n0 + OUT_TN], preferred_element_type=F32)

    @pl.when(j == pl.num_programs(1) - 1)
    def _():
        half_gate = 0.5 * mod[2:3]
        rows = x_ref.shape[0]
        rc = min(EPI_ROWS, rows)

        def body(r, carry):
            sl = pl.ds(pl.multiple_of(r * rc, rc), rc)
            y = x_ref[sl, :] + half_gate * o_ref[sl, :]
            if final:
                ms = jnp.mean(y * y, axis=-1, keepdims=True)
                y = y * lax.rsqrt(ms + RMS_EPS) * fg_ref[...]
            o_ref[sl, :] = y
            return carry

        lax.fori_loop(0, rows // rc, body, 0)


def _ffn(x, g, mod3, w_in, w_out, layer, sub, rows_per_mod, tm, final_g=None):
    m, d = x.shape
    f = w_out.shape[2]
    nj = -(-f // FFN_TF)
    tpm = rows_per_mod // tm
    final = final_g is not None
    in_specs = [
        pl.BlockSpec((tm, d), lambda i, j: (i, 0)),
        pl.BlockSpec((1, d), lambda i, j: (0, 0)),
        pl.BlockSpec((1, 3, d), lambda i, j: (i // tpm, 0, 0)),
        pl.BlockSpec((pl.Element(1), pl.Element(1), pl.Element(d), pl.Element(FFN_TF)),
                     lambda i, j: (layer, sub, 0, _ffn_tile_start(j, f))),
        pl.BlockSpec((pl.Element(1), pl.Element(1), pl.Element(d), pl.Element(FFN_TF)),
                     lambda i, j: (layer, sub, 0, pl.multiple_of(f + _ffn_tile_start(j, f), LANES))),
        pl.BlockSpec((pl.Element(1), pl.Element(1), pl.Element(FFN_TF), pl.Element(d)),
                     lambda i, j: (layer, sub, _ffn_tile_start(j, f), 0)),
    ]
    args = [x, g.reshape(1, d), mod3, w_in, w_in, w_out]
    if final:
        in_specs.append(pl.BlockSpec((1, d), lambda i, j: (0, 0)))
        args.append(final_g.reshape(1, d))
    return pl.pallas_call(
        functools.partial(_ffn_kernel, f=f, final=final),
        grid=(m // tm, nj),
        in_specs=in_specs,
        out_specs=pl.BlockSpec((tm, d), lambda i, j: (i, 0)),
        out_shape=jax.ShapeDtypeStruct((m, d), F32),
        scratch_shapes=[pltpu.VMEM((tm, d), BF16)],
        compiler_params=_params(("parallel", "arbitrary")),
        name="ffn_final" if final else "ffn",
    )(*args)


def _proj_kernel(x_ref, g_ref, mod_ref, w_ref, wt_ref, wg_ref, bg_ref,
                 main_ref, tr_ref, gates_ref, xn_ref, *, n_main, n_tr):
    j = pl.program_id(1)
    steps = max(n_main, n_tr)

    @pl.when(j == 0)
    def _():
        _fill_xn(x_ref, xn_ref, g_ref[...], mod_ref[0])
        gates_ref[...] = jnp.dot(xn_ref[...], wg_ref[...], preferred_element_type=F32) + bg_ref[...]

    def main_tile():
        main_ref[...] = jnp.dot(xn_ref[...], w_ref[...], preferred_element_type=F32).astype(BF16)

    def tr_tile():
        tr = lax.dot_general(wt_ref[...], xn_ref[...], (((1,), (1,)), ((), ())), preferred_element_type=F32)
        tr_ref[0] = tr.astype(BF16)

    for tiles, emit in ((n_main, main_tile), (n_tr, tr_tile)):
        if tiles == steps:
            emit()
        else:
            pl.when(j < tiles)(emit)


def _proj(x, g, mod3, w_main, w_tr, w_gate, b_gate, rows_per_batch, tm):
    m, d = x.shape
    n = w_main.shape[1]
    nt = w_tr.shape[0]
    n_main, n_tr = n // PROJ_TN, nt // PROJ_TN
    batch = m // rows_per_batch
    tpb = rows_per_batch // tm
    nmod = mod3.shape[0]
    mod_idx = (lambda i, j: (i // tpb, 0, 0)) if nmod > 1 else (lambda i, j: (0, 0, 0))
    return pl.pallas_call(
        functools.partial(_proj_kernel, n_main=n_main, n_tr=n_tr),
        grid=(m // tm, max(n_main, n_tr)),
        in_specs=[
            pl.BlockSpec((tm, d), lambda i, j: (i, 0)),
            pl.BlockSpec((1, d), lambda i, j: (0, 0)),
            pl.BlockSpec((1, 3, d), mod_idx),
            pl.BlockSpec((d, PROJ_TN), lambda i, j: (0, jnp.minimum(j, n_main - 1))),
            pl.BlockSpec((PROJ_TN, d), lambda i, j: (jnp.minimum(j, n_tr - 1), 0)),
            pl.BlockSpec((d, LANES), lambda i, j: (0, 0)),
            pl.BlockSpec((1, LANES), lambda i, j: (0, 0)),
        ],
        out_specs=[
            pl.BlockSpec((tm, PROJ_TN), lambda i, j: (i, jnp.minimum(j, n_main - 1))),
            pl.BlockSpec((1, PROJ_TN, tm), lambda i, j: (i // tpb, jnp.minimum(j, n_tr - 1), i % tpb)),
            pl.BlockSpec((tm, LANES), lambda i, j: (i, 0)),
        ],
        out_shape=[
            jax.ShapeDtypeStruct((m, n), BF16),
            jax.ShapeDtypeStruct((batch, nt, rows_per_batch), BF16),
            jax.ShapeDtypeStruct((m, LANES), F32),
        ],
        scratch_shapes=[pltpu.VMEM((tm, d), BF16)],
        compiler_params=_params(("parallel", "arbitrary")),
        name="mlstm_proj",
    )(x, g.reshape(1, d), mod3, w_main, w_tr, w_gate, b_gate)


def _tri_sum(tri, x):
    x1 = x.astype(BF16)
    r1 = x - x1.astype(F32)
    x2 = r1.astype(BF16)
    x3 = (r1 - x2.astype(F32)).astype(BF16)
    dot = functools.partial(jnp.dot, preferred_element_type=F32)
    return dot(tri, x1) + dot(tri, x2) + dot(tri, x3)


def _gateprep_kernel(g_ref, pcol_ref, prow_ref, *, chunk):
    t = g_ref.shape[1]
    row = lax.broadcasted_iota(jnp.int32, (chunk, chunk), 0)
    col = lax.broadcasted_iota(jnp.int32, (chunk, chunk), 1)
    tril = jnp.where(row >= col, 1.0, 0.0).astype(BF16)
    triu = jnp.where(row <= col, 1.0, 0.0).astype(BF16)
    lane = lax.broadcasted_iota(jnp.int32, (chunk, LANES), 1)
    f_fwd = (lane >= HEADS) & (lane < 2 * HEADS)
    f_bwd = (lane >= 3 * HEADS) & (lane < 4 * HEADS)
    tok = lax.broadcasted_iota(jnp.int32, (chunk, LANES), 0)
    run_lanes = ((lane >= RUNMAX_LANE) & (lane < RUNMAX_LANE + HEADS)) | (
        (lane >= RUNMAX_LANE + 2 * HEADS) & (lane < RUNMAX_LANE + 3 * HEADS))

    def body(ci, carry):
        sl = pl.ds(pl.multiple_of(ci * chunk, chunk), chunk)
        g = g_ref[0, sl, :]
        logf = jnp.minimum(g, 0.0) - jnp.log(1.0 + jnp.exp(-jnp.abs(g)))
        logf = jnp.where(f_fwd | f_bwd, logf, 0.0)
        cum = jnp.where(f_fwd, _tri_sum(tril, logf), jnp.where(f_bwd, _tri_sum(triu, logf), 0.0))
        p = jnp.where(f_fwd | f_bwd, cum, g - pltpu.roll(cum, LANES - HEADS, 1))
        pre, suf = p, p
        for sh in [1 << b for b in range(chunk.bit_length() - 1)]:
            pre = jnp.maximum(pre, jnp.where(tok >= sh, pltpu.roll(pre, sh, 0), -jnp.inf))
            suf = jnp.maximum(suf, jnp.where(tok < chunk - sh, pltpu.roll(suf, chunk - sh, 0), -jnp.inf))
        run = jnp.where(lane < 2 * HEADS, pre, suf)
        p = jnp.where(run_lanes, pltpu.roll(run, RUNMAX_LANE, 1), p)
        pcol_ref[0, sl, :] = p
        prow_ref[0, :, sl] = p.T
        return carry

    lax.fori_loop(0, t // chunk, body, 0)


def _gateprep(gates, batch, chunk):
    m = gates.shape[0]
    t = m // batch
    g3 = gates.reshape(batch, t, LANES)
    return pl.pallas_call(
        functools.partial(_gateprep_kernel, chunk=chunk),
        grid=(batch,),
        in_specs=[pl.BlockSpec((1, t, LANES), lambda b: (b, 0, 0))],
        out_specs=[
            pl.BlockSpec((1, t, LANES), lambda b: (b, 0, 0)),
            pl.BlockSpec((1, LANES, t), lambda b: (b, 0, 0)),
        ],
        out_shape=[
            jax.ShapeDtypeStruct((batch, t, LANES), F32),
            jax.ShapeDtypeStruct((batch, LANES, t), F32),
        ],
        compiler_params=_params(("parallel",)),
        name="gate_prep",
    )(g3)


def _scan_kernel(k_ref, o_ref, qt_ref, vt_ref, pcol_ref, prow_ref,
                 kc_ref, vtc_ref, prowc_ref, ng_ref,
                 y_ref, s_ref, min_ref, *, chunk, dk, dv):
    head = pl.program_id(1)
    t = k_ref.shape[1]
    tc = kc_ref.shape[1]
    nc = t // chunk
    ncc = tc // chunk
    lane = lax.broadcasted_iota(jnp.int32, (chunk, LANES), 1)
    row = lax.broadcasted_iota(jnp.int32, (chunk, chunk), 0)
    colm = lax.broadcasted_iota(jnp.int32, (chunk, chunk), 1)
    ones_t = jnp.ones((LANES, chunk), BF16)
    dot = functools.partial(jnp.dot, preferred_element_type=F32)

    def column(p, idx):
        return jnp.sum(jnp.where(lane == idx, p, 0.0), axis=1, keepdims=True)

    directions = ((0, 0, chunk - 1), (1, 2 * HEADS, 0))

    def gate_rows(pr_ref, t0, base):
        a_row = pr_ref[0, pl.ds(base + head, 1), pl.ds(t0, chunk)]
        c_row = pr_ref[0, pl.ds(base + HEADS + head, 1), pl.ds(t0, chunk)]
        return a_row, c_row

    def aug_t(vt):
        return jnp.concatenate([vt, ones_t], axis=0)

    def local_sum(k, vt, a_row, tot):
        w_end = tot + a_row
        mloc = jnp.max(w_end, axis=1, keepdims=True)
        w = jnp.exp(w_end - mloc)
        wv_t = jnp.concatenate([(vt.astype(F32) * w).astype(BF16),
                                jnp.broadcast_to(w, (LANES, chunk)).astype(BF16)], axis=0)
        return dot(wv_t, k), mloc

    def merge(caug, m, tot, g, mloc):
        m_new = jnp.maximum(tot + m, mloc)
        return jnp.exp(tot + m - m_new) * caug + jnp.exp(mloc - m_new) * g, m_new

    def scalar_tile(x):
        return jnp.broadcast_to(x, (8, LANES))

    states = []
    for d, base, last in directions:
        caug, m = jnp.zeros((dv + LANES, dk), F32), jnp.full((1, 1), M_INIT, F32)
        for ci in (range(ncc) if d == 0 else reversed(range(ncc))):
            t0 = ci * chunk
            a_row, c_row = gate_rows(prowc_ref, t0, base)
            tot = c_row[:, last:last + 1]
            g, mloc = local_sum(kc_ref[0, pl.ds(t0, chunk), :], vtc_ref[0, :, pl.ds(t0, chunk)], a_row, tot)
            caug, m = merge(caug, m, tot, g, mloc)
        states += [caug, m]

    def state_pass(step, carry):
        carry = list(carry)
        for d, base, last in directions:
            ci = step if d == 0 else nc - 1 - step
            t0 = pl.multiple_of(ci * chunk, chunk)
            caug, m = carry[2 * d], carry[2 * d + 1]
            s_ref[d, ci] = caug.astype(BF16)
            min_ref[d, ci] = scalar_tile(m)
            a_row, c_row = gate_rows(prow_ref, t0, base)
            tot = c_row[:, last:last + 1]
            g, mloc = local_sum(k_ref[0, pl.ds(t0, chunk), :], vt_ref[0, :, pl.ds(t0, chunk)], a_row, tot)
            carry[2 * d], carry[2 * d + 1] = merge(caug, m, tot, g, mloc)
        return tuple(carry)

    lax.fori_loop(0, nc, state_pass, tuple(states), unroll=SCAN_UNROLL_A)

    masks = (row <= colm, row >= colm)
    ng = ng_ref[...]

    def pass_c(ci, carry):
        t0 = pl.multiple_of(ci * chunk, chunk)
        sl = pl.ds(t0, chunk)
        qt = qt_ref[0, :, sl]
        vaug_t = aug_t(vt_ref[0, :, sl])
        s_t = dot(k_ref[0, sl, :], qt)
        pc = pcol_ref[0, sl, :]
        h_t = None
        for d, base, last in directions:
            a_col = column(pc, base + head)
            _, c_row = gate_rows(prow_ref, t0, base)
            run_row = prow_ref[0, pl.ds(RUNMAX_LANE + base + head, 1), sl]
            inter = c_row + min_ref[d, ci][0:1, 0:1]
            m_out = jnp.maximum(inter, c_row + run_row)
            p_t = (s_t * jnp.exp(jnp.where(masks[d], a_col + (c_row - m_out), -jnp.inf))).astype(BF16)
            qs = (qt.astype(F32) * jnp.exp(inter - m_out)).astype(BF16)
            r_t = dot(vaug_t, p_t) + dot(s_ref[d, ci], qs)
            inv = 1.0 / jnp.maximum(jnp.abs(r_t[dv:dv + 1]), jnp.exp(-m_out))
            hd = r_t[:dv] * inv
            h_t = hd if h_t is None else h_t + hd
        hn_t = h_t * lax.rsqrt(jnp.mean(h_t * h_t, axis=0, keepdims=True) + RMS_EPS)
        y_ref[0, sl, :] = (hn_t.T * ng * _sigmoid(o_ref[0, sl, :].astype(F32))).astype(BF16)
        return carry

    lax.fori_loop(0, nc, pass_c, 0, unroll=SCAN_UNROLL_C)


def _scan(main, tr, pcol, prow, main_c, tr_c, prowc, norm_g, batch, t, tc, d):
    dv = d // HEADS
    dk = dv // 2
    wide0 = HEADS * dk // dv
    main3 = main.reshape(batch, t, main.shape[1])
    main_c3 = main_c.reshape(batch, tc, main_c.shape[1])
    nc = t // SCAN_L
    return pl.pallas_call(
        functools.partial(_scan_kernel, chunk=SCAN_L, dk=dk, dv=dv),
        grid=(batch, HEADS),
        in_specs=[
            pl.BlockSpec((1, t, dk), lambda b, h: (b, 0, h)),
            pl.BlockSpec((1, t, dv), lambda b, h: (b, 0, wide0 + h)),
            pl.BlockSpec((1, dk, t), lambda b, h: (b, h, 0)),
            pl.BlockSpec((1, dv, t), lambda b, h: (b, wide0 + h, 0)),
            pl.BlockSpec((1, t, LANES), lambda b, h: (b, 0, 0)),
            pl.BlockSpec((1, LANES, t), lambda b, h: (b, 0, 0)),
            pl.BlockSpec((1, tc, dk), lambda b, h: (b, 0, h)),
            pl.BlockSpec((1, dv, tc), lambda b, h: (b, h, 0)),
            pl.BlockSpec((1, LANES, tc), lambda b, h: (b, 0, 0)),
            pl.BlockSpec((1, dv), lambda b, h: (0, h)),
        ],
        out_specs=pl.BlockSpec((1, t, dv), lambda b, h: (b, 0, h)),
        out_shape=jax.ShapeDtypeStruct((batch, t, d), BF16),
        scratch_shapes=[
            pltpu.VMEM((2, nc, dv + LANES, dk), BF16),
            pltpu.VMEM((2, nc, 8, LANES), F32),
        ],
        compiler_params=_params(("parallel", "arbitrary")),
        name="mlstm_scan",
    )(main3, main3, tr, tr, pcol, prow, main_c3, tr_c, prowc, norm_g.reshape(1, d))


def _outproj_kernel(y_ref, x_ref, mod_ref, w_ref, o_ref):
    gate = mod_ref[0][2:3]
    o_ref[...] = x_ref[...] + gate * jnp.dot(y_ref[...], w_ref[...], preferred_element_type=F32)


def _outproj(y, x, mod3, w, rows_per_mod, tm):
    m, d = x.shape
    tpm = rows_per_mod // tm
    return pl.pallas_call(
        _outproj_kernel,
        grid=(m // tm,),
        in_specs=[
            pl.BlockSpec((tm, d), lambda i: (i, 0)),
            pl.BlockSpec((tm, d), lambda i: (i, 0)),
            pl.BlockSpec((1, 3, d), lambda i: (i // tpm, 0, 0)),
            pl.BlockSpec((d, d), lambda i: (0, 0)),
        ],
        out_specs=pl.BlockSpec((tm, d), lambda i: (i, 0)),
        out_shape=jax.ShapeDtypeStruct((m, d), F32),
        compiler_params=_params(("parallel",)),
        name="mlstm_out",
    )(y, x, mod3, w)


def _conv_kernel(x_ref, g_ref, mod_ref, wb_ref, wc_ref, wu_ref, cw_ref, wo_ref, o_ref, xn_ref, *, tn):
    j = pl.program_id(1)
    mod = mod_ref[0]
    tm = x_ref.shape[0]

    @pl.when(j == 0)
    def _():
        _fill_xn(x_ref, xn_ref, g_ref[...], mod)
        o_ref[...] = jnp.zeros_like(o_ref)

    xn = xn_ref[...]
    bg = jnp.dot(xn, wb_ref[...], preferred_element_type=F32)
    cg = jnp.dot(xn, wc_ref[...], preferred_element_type=F32)
    u = jnp.dot(xn, wu_ref[...], preferred_element_type=F32)
    z = cg * u
    pos = lax.broadcasted_iota(jnp.int32, (tm, tn), 0) % GRID_W
    z_prev = jnp.where(pos == 0, 0.0, pltpu.roll(z, 1, 0))
    z_next = jnp.where(pos == GRID_W - 1, 0.0, pltpu.roll(z, tm - 1, 0))
    cw = cw_ref[...]
    zc = cw[0:1] * z_prev + cw[1:2] * z + cw[2:3] * z_next
    a = (bg * zc).astype(BF16)
    o_ref[...] += jnp.dot(a, wo_ref[...], preferred_element_type=F32)

    @pl.when(j == pl.num_programs(1) - 1)
    def _():
        o_ref[...] = x_ref[...] + mod[2:3] * o_ref[...]


def _conv(x, g, mod3, w3, cw, wo, rows_per_mod, tm):
    m, d = x.shape
    nj = d // CONV_TN
    tpm = rows_per_mod // tm
    return pl.pallas_call(
        functools.partial(_conv_kernel, tn=CONV_TN),
        grid=(m // tm, nj),
        in_specs=[
            pl.BlockSpec((tm, d), lambda i, j: (i, 0)),
            pl.BlockSpec((1, d), lambda i, j: (0, 0)),
            pl.BlockSpec((1, 3, d), lambda i, j: (i // tpm, 0, 0)),
            pl.BlockSpec((d, CONV_TN), lambda i, j: (0, j)),
            pl.BlockSpec((d, CONV_TN), lambda i, j: (0, nj + j)),
            pl.BlockSpec((d, CONV_TN), lambda i, j: (0, 2 * nj + j)),
            pl.BlockSpec((3, CONV_TN), lambda i, j: (0, j)),
            pl.BlockSpec((CONV_TN, d), lambda i, j: (j, 0)),
        ],
        out_specs=pl.BlockSpec((tm, d), lambda i, j: (i, 0)),
        out_shape=jax.ShapeDtypeStruct((m, d), F32),
        scratch_shapes=[pltpu.VMEM((tm, d), BF16)],
        compiler_params=_params(("parallel", "arbitrary")),
        name="conv_mixer",
    )(x, g.reshape(1, d), mod3, w3, w3, w3, cw, wo)


def kernel(x, c, ctx, c_ctx, w_mod, b_mod, norm_g, ffn_w_in, ffn_w_out, mlstm_w_in, mlstm_b_gate,
           mlstm_norm_g, mlstm_w_out, conv_w_in, conv_w, conv_w_out, final_norm_g):
    batch, t, d = x.shape
    tc = ctx.shape[1]
    depth = w_mod.shape[0]
    assert depth == 2 and d % (2 * HEADS * LANES) == 0 and t % GRID_W == 0
    dv = d // HEADS
    dk = dv // 2
    qk = HEADS * dk

    c_rows = jnp.concatenate([c, c_ctx[None, :], jnp.zeros((-(batch + 1) % 8, d), F32)], axis=0)
    mod = _modulation(c_rows, w_mod, b_mod).reshape(depth, c_rows.shape[0], N_MOD, d)

    def mod3(layer, sub, context=False):
        rows = mod[layer, batch:batch + 1] if context else mod[layer, :batch]
        return rows[:, 3 * sub:3 * sub + 3, :]

    tm = min(ROW_TM, t)
    tmix = min(MIX_TM, t)
    xf = x.reshape(batch * t, d)
    cf = ctx.reshape(batch * tc, d)

    ffn_in = ffn_w_in.astype(BF16)
    ffn_out = ffn_w_out.astype(BF16)
    xf = _ffn(xf, norm_g[0, 0], mod3(0, 0), ffn_in, ffn_out, 0, 0, t, tm)
    cf = _ffn(cf, norm_g[0, 0], mod3(0, 0, True), ffn_in, ffn_out, 0, 0, batch * tc, tm)

    w_in = mlstm_w_in[0]
    w_qt = (w_in[:, :qk] * (dk ** -0.5)).astype(BF16).T
    w_k = w_in[:, qk:2 * qk].astype(BF16)
    w_vt = w_in[:, 2 * qk:2 * qk + d].astype(BF16).T
    g0 = 2 * qk + d
    ng = 4 * HEADS
    w_gate = jnp.pad(w_in[:, g0:g0 + ng].astype(BF16), ((0, 0), (0, LANES - ng)))
    b_gate = jnp.pad(mlstm_b_gate[0], (0, LANES - ng)).reshape(1, LANES)
    w_o = w_in[:, g0 + ng:].astype(BF16)
    main, tr, gates = _proj(xf, norm_g[0, 1], mod3(0, 1), jnp.concatenate([w_k, w_o], axis=1),
                            jnp.concatenate([w_qt, w_vt], axis=0), w_gate, b_gate, t, min(PROJ_TM, t))
    main_c, tr_c, gates_c = _proj(cf, norm_g[0, 1], mod3(0, 1, True), w_k, w_vt, w_gate, b_gate,
                                  tc, min(PROJ_TM, tc))
    pcol, prow = _gateprep(gates, batch, SCAN_L)
    _, prowc = _gateprep(gates_c, batch, SCAN_L)
    y = _scan(main, tr, pcol, prow, main_c, tr_c, prowc, mlstm_norm_g[0], batch, t, tc, d)
    xf = _outproj(y.reshape(batch * t, d), xf, mod3(0, 1), mlstm_w_out[0].astype(BF16), t, tmix)

    xf = _ffn(xf, norm_g[0, 2], mod3(0, 2), ffn_in, ffn_out, 0, 1, t, tm)

    xf = _ffn(xf, norm_g[1, 0], mod3(1, 0), ffn_in, ffn_out, 1, 0, t, tm)
    xf = _conv(xf, norm_g[1, 1], mod3(1, 1), conv_w_in[0].astype(BF16), conv_w[0],
               conv_w_out[0].astype(BF16), t, tmix)
    xf = _ffn(xf, norm_g[1, 2], mod3(1, 2), ffn_in, ffn_out, 1, 1, t, tm,
              final_g=final_norm_g)
    return xf.reshape(batch, t, d)
```

```python
import functools

import jax
import jax.numpy as jnp
from jax import lax
from jax.experimental import pallas as pl
from jax.experimental.pallas import tpu as pltpu

F32 = jnp.float32
BF16 = jnp.bfloat16

HEADS = 8
N_MOD = 9
GRID_W = 64
RMS_EPS = 1e-6
M_INIT = -1e30

LANES = 128
MXU_COLS = 256
VMEM_LIMIT_BYTES = 60 * 1024 * 1024

NORM_ROWS = 64
EPI_ROWS = 256
ROW_TM = 1024
MIX_TM = 512
ROW_GROUPS = 4
OUT_TN = 512
FFN_TF = 512
CONV_TN = 512
PROJ_TN = 1024
PROJ_TM = 1024
SCAN_L = 256
RUNMAX_LANE = 4 * HEADS
SCAN_UNROLL_A = 4
SCAN_UNROLL_C = 4
MOD_TN = 1024


def _params(sem):
    return pltpu.CompilerParams(dimension_semantics=sem, vmem_limit_bytes=VMEM_LIMIT_BYTES)


def _sigmoid(x):
    return 1.0 / (1.0 + jnp.exp(-x))


def _norm_rows(x_ref, start, rows, gain, shift):
    rc = min(NORM_ROWS, rows)
    out = []
    for r0 in range(start, start + rows, rc):
        x = x_ref[r0:r0 + rc, :]
        inv = lax.rsqrt(jnp.mean(x * x, axis=-1, keepdims=True) + RMS_EPS)
        out.append((x_ref[r0:r0 + rc, :] * inv * gain + shift).astype(BF16))
    return jnp.concatenate(out, axis=0)


def _fill_xn(x_ref, xn_ref, g, mod):
    rows = x_ref.shape[0]
    rc = min(NORM_ROWS, rows)
    shift = mod[0:1]
    gain = g * (1.0 + mod[1:2])

    n = rows // rc

    def inv_rms(r):
        x = x_ref[pl.ds(pl.multiple_of(r * rc, rc), rc), :]
        return lax.rsqrt(jnp.mean(x * x, axis=-1, keepdims=True) + RMS_EPS)

    def body(r, inv):
        inv_next = inv_rms(jnp.minimum(r + 1, n - 1))
        sl = pl.ds(pl.multiple_of(r * rc, rc), rc)
        xn_ref[sl, :] = (x_ref[sl, :] * inv * gain + shift).astype(BF16)
        return inv_next

    lax.fori_loop(0, n, body, inv_rms(0))


def _mod_kernel(c_ref, w_ref, b_ref, o_ref):
    c = c_ref[...]
    s = (c * _sigmoid(c)).astype(BF16)
    w = w_ref[0].astype(BF16)
    o_ref[0] = jnp.dot(s, w, preferred_element_type=F32) + b_ref[0]


def _modulation(c_rows, w_mod, b_mod):
    depth, d, n = w_mod.shape
    r = c_rows.shape[0]
    return pl.pallas_call(
        _mod_kernel,
        grid=(depth, n // MOD_TN),
        in_specs=[
            pl.BlockSpec((r, d), lambda l, j: (0, 0)),
            pl.BlockSpec((1, d, MOD_TN), lambda l, j: (l, 0, j)),
            pl.BlockSpec((1, 1, MOD_TN), lambda l, j: (l, 0, j)),
        ],
        out_specs=pl.BlockSpec((1, r, MOD_TN), lambda l, j: (l, 0, j)),
        out_shape=jax.ShapeDtypeStruct((depth, r, n), F32),
        compiler_params=_params(("arbitrary", "arbitrary")),
        name="modulation",
    )(c_rows, w_mod, b_mod.reshape(depth, 1, n))


def _ffn_tile_start(j, f):
    assert f % LANES == 0 and f >= FFN_TF
    return pl.multiple_of(jnp.minimum(j * FFN_TF, f - FFN_TF), LANES)


def _ffn_kernel(*refs, f, final):
    if final:
        x_ref, g_ref, mod_ref, wg_ref, wu_ref, wo_ref, fg_ref, o_ref, xn_ref = refs
    else:
        x_ref, g_ref, mod_ref, wg_ref, wu_ref, wo_ref, o_ref, xn_ref = refs
    j = pl.program_id(1)
    nj = pl.num_programs(1)
    mod = mod_ref[0]
    shift, half_gate = mod[0:1], 0.5 * mod[2:3]
    gain = g_ref[...] * (1.0 + mod[1:2])
    d = o_ref.shape[1]
    rg = x_ref.shape[0] // ROW_GROUPS
    col = lax.broadcasted_iota(jnp.int32, (rg, FFN_TF), 1)

    def step(first, last):
        done = j * FFN_TF - _ffn_tile_start(j, f)
        acts = []
        for r in range(ROW_GROUPS):
            rows = slice(r * rg, (r + 1) * rg)
            if first:
                xn = _norm_rows(x_ref, r * rg, rg, gain, shift)
                xn_ref[rows, :] = xn
            else:
                xn = xn_ref[rows, :]
            hg = jnp.dot(xn, wg_ref[0, 0], preferred_element_type=F32)
            hu = jnp.dot(xn, wu_ref[0, 0], preferred_element_type=F32)
            acts.append((hg * _sigmoid(hg) * jnp.where(col >= done, hu, 0.0)).astype(BF16))
        for r in range(ROW_GROUPS):
            rows = slice(r * rg, (r + 1) * rg)
            for n0 in range(0, d, OUT_TN):
                cols = slice(n0, n0 + OUT_TN)
                upd = jnp.dot(acts[r], wo_ref[0, 0, :, cols], preferred_element_type=F32)
                acc = upd if first else o_ref[rows, cols] + upd
                if last:
                    acc = x_ref[rows, cols] + half_gate[:, cols] * acc
                o_ref[rows, cols] = acc
            if last and final:
                y = o_ref[rows, :]
                ms = jnp.mean(y * y, axis=-1, keepdims=True)
                o_ref[rows, :] = y * lax.rsqrt(ms + RMS_EPS) * fg_ref[...]

    pl.when(j == 0)(lambda: step(True, False))
    pl.when((j > 0) & (j < nj - 1))(lambda: step(False, False))
    pl.when(j == nj - 1)(lambda: step(False, True))


def _ffn(x, g, mod3, w_in, w_out, layer, sub, rows_per_mod, tm, final_g=None):
    m, d = x.shape
    f = w_out.shape[2]
    nj = -(-f // FFN_TF)
    tpm = rows_per_mod // tm
    final = final_g is not None
    in_specs = [
        pl.BlockSpec((tm, d), lambda i, j: (i, 0)),
        pl.BlockSpec((1, d), lambda i, j: (0, 0)),
        pl.BlockSpec((1, 3, d), lambda i, j: (i // tpm, 0, 0)),
        pl.BlockSpec((pl.Element(1), pl.Element(1), pl.Element(d), pl.Element(FFN_TF)),
                     lambda i, j: (layer, sub, 0, _ffn_tile_start(j, f))),
        pl.BlockSpec((pl.Element(1), pl.Element(1), pl.Element(d), pl.Element(FFN_TF)),
                     lambda i, j: (layer, sub, 0, pl.multiple_of(f + _ffn_tile_start(j, f), LANES))),
        pl.BlockSpec((pl.Element(1), pl.Element(1), pl.Element(FFN_TF), pl.Element(d)),
                     lambda i, j: (layer, sub, _ffn_tile_start(j, f), 0)),
    ]
    args = [x, g.reshape(1, d), mod3, w_in, w_in, w_out]
    if final:
        in_specs.append(pl.BlockSpec((1, d), lambda i, j: (0, 0)))
        args.append(final_g.reshape(1, d))
    return pl.pallas_call(
        functools.partial(_ffn_kernel, f=f, final=final),
        grid=(m // tm, nj),
        in_specs=in_specs,
        out_specs=pl.BlockSpec((tm, d), lambda i, j: (i, 0)),
        out_shape=jax.ShapeDtypeStruct((m, d), F32),
        scratch_shapes=[pltpu.VMEM((tm, d), BF16)],
        compiler_params=_params(("parallel", "arbitrary")),
        name="ffn_final" if final else "ffn",
    )(*args)


def _proj_kernel(x_ref, g_ref, mod_ref, w_ref, wt_ref, wg_ref, bg_ref,
                 main_ref, tr_ref, gates_ref, xn_ref, *, n_main, n_tr):
    j = pl.program_id(1)
    mod = mod_ref[0]
    rg = x_ref.shape[0] // ROW_GROUPS

    @pl.when(j == 0)
    def _():
        gain = g_ref[...] * (1.0 + mod[1:2])
        for r in range(ROW_GROUPS):
            rows = slice(r * rg, (r + 1) * rg)
            xn = _norm_rows(x_ref, r * rg, rg, gain, mod[0:1])
            xn_ref[rows, :] = xn
            gates_ref[rows, :] = jnp.dot(xn, wg_ref[...], preferred_element_type=F32) + bg_ref[...]
            main_ref[rows, :] = jnp.dot(xn, w_ref[...], preferred_element_type=F32).astype(BF16)
            tr = lax.dot_general(wt_ref[...], xn, (((1,), (1,)), ((), ())), preferred_element_type=F32)
            tr_ref[0, :, rows] = tr.astype(BF16)

    def main_tile():
        main_ref[...] = jnp.dot(xn_ref[...], w_ref[...], preferred_element_type=F32).astype(BF16)

    def tr_tile():
        tr = lax.dot_general(wt_ref[...], xn_ref[...], (((1,), (1,)), ((), ())), preferred_element_type=F32)
        tr_ref[0] = tr.astype(BF16)

    def both_tiles():
        main_tile()
        tr_tile()

    both = min(n_main, n_tr)
    if both > 1:
        pl.when((j > 0) & (j < both))(both_tiles)
    if n_main != n_tr:
        pl.when(j >= both)(main_tile if n_main > n_tr else tr_tile)


def _proj(x, g, mod3, w_main, w_tr, w_gate, b_gate, rows_per_batch, tm):
    m, d = x.shape
    n = w_main.shape[1]
    nt = w_tr.shape[0]
    n_main, n_tr = n // PROJ_TN, nt // PROJ_TN
    batch = m // rows_per_batch
    tpb = rows_per_batch // tm
    nmod = mod3.shape[0]
    mod_idx = (lambda i, j: (i // tpb, 0, 0)) if nmod > 1 else (lambda i, j: (0, 0, 0))
    return pl.pallas_call(
        functools.partial(_proj_kernel, n_main=n_main, n_tr=n_tr),
        grid=(m // tm, max(n_main, n_tr)),
        in_specs=[
            pl.BlockSpec((tm, d), lambda i, j: (i, 0)),
            pl.BlockSpec((1, d), lambda i, j: (0, 0)),
            pl.BlockSpec((1, 3, d), mod_idx),
            pl.BlockSpec((d, PROJ_TN), lambda i, j: (0, jnp.minimum(j, n_main - 1))),
            pl.BlockSpec((PROJ_TN, d), lambda i, j: (jnp.minimum(j, n_tr - 1), 0)),
            pl.BlockSpec((d, LANES), lambda i, j: (0, 0)),
            pl.BlockSpec((1, LANES), lambda i, j: (0, 0)),
        ],
        out_specs=[
            pl.BlockSpec((tm, PROJ_TN), lambda i, j: (i, jnp.minimum(j, n_main - 1))),
            pl.BlockSpec((1, PROJ_TN, tm), lambda i, j: (i // tpb, jnp.minimum(j, n_tr - 1), i % tpb)),
            pl.BlockSpec((tm, LANES), lambda i, j: (i, 0)),
        ],
        out_shape=[
            jax.ShapeDtypeStruct((m, n), BF16),
            jax.ShapeDtypeStruct((batch, nt, rows_per_batch), BF16),
            jax.ShapeDtypeStruct((m, LANES), F32),
        ],
        scratch_shapes=[pltpu.VMEM((tm, d), BF16)],
        compiler_params=_params(("parallel", "arbitrary")),
        name="mlstm_proj",
    )(x, g.reshape(1, d), mod3, w_main, w_tr, w_gate, b_gate)


def _tri_sum(tri, x):
    x1 = x.astype(BF16)
    r1 = x - x1.astype(F32)
    x2 = r1.astype(BF16)
    x3 = (r1 - x2.astype(F32)).astype(BF16)
    dot = functools.partial(jnp.dot, preferred_element_type=F32)
    return dot(tri, x1) + dot(tri, x2) + dot(tri, x3)


def _gateprep_kernel(g_ref, pcol_ref, prow_ref, *, chunk):
    t = g_ref.shape[1]
    row = lax.broadcasted_iota(jnp.int32, (chunk, chunk), 0)
    col = lax.broadcasted_iota(jnp.int32, (chunk, chunk), 1)
    tril = jnp.where(row >= col, 1.0, 0.0).astype(BF16)
    triu = jnp.where(row <= col, 1.0, 0.0).astype(BF16)
    lane = lax.broadcasted_iota(jnp.int32, (chunk, LANES), 1)
    f_fwd = (lane >= HEADS) & (lane < 2 * HEADS)
    f_bwd = (lane >= 3 * HEADS) & (lane < 4 * HEADS)
    tok = lax.broadcasted_iota(jnp.int32, (chunk, LANES), 0)
    run_lanes = ((lane >= RUNMAX_LANE) & (lane < RUNMAX_LANE + HEADS)) | (
        (lane >= RUNMAX_LANE + 2 * HEADS) & (lane < RUNMAX_LANE + 3 * HEADS))

    def body(ci, carry):
        sl = pl.ds(pl.multiple_of(ci * chunk, chunk), chunk)
        g = g_ref[0, sl, :]
        logf = jnp.minimum(g, 0.0) - jnp.log(1.0 + jnp.exp(-jnp.abs(g)))
        logf = jnp.where(f_fwd | f_bwd, logf, 0.0)
        cum = jnp.where(f_fwd, _tri_sum(tril, logf), jnp.where(f_bwd, _tri_sum(triu, logf), 0.0))
        p = jnp.where(f_fwd | f_bwd, cum, g - pltpu.roll(cum, LANES - HEADS, 1))
        pre, suf = p, p
        for sh in [1 << b for b in range(chunk.bit_length() - 1)]:
            pre = jnp.maximum(pre, jnp.where(tok >= sh, pltpu.roll(pre, sh, 0), -jnp.inf))
            suf = jnp.maximum(suf, jnp.where(tok < chunk - sh, pltpu.roll(suf, chunk - sh, 0), -jnp.inf))
        run = jnp.where(lane < 2 * HEADS, pre, suf)
        p = jnp.where(run_lanes, pltpu.roll(run, RUNMAX_LANE, 1), p)
        pcol_ref[0, sl, :] = p
        prow_ref[0, :, sl] = p.T
        return carry

    lax.fori_loop(0, t // chunk, body, 0)


def _gateprep(gates, batch, chunk):
    m = gates.shape[0]
    t = m // batch
    g3 = gates.reshape(batch, t, LANES)
    return pl.pallas_call(
        functools.partial(_gateprep_kernel, chunk=chunk),
        grid=(batch,),
        in_specs=[pl.BlockSpec((1, t, LANES), lambda b: (b, 0, 0))],
        out_specs=[
            pl.BlockSpec((1, t, LANES), lambda b: (b, 0, 0)),
            pl.BlockSpec((1, LANES, t), lambda b: (b, 0, 0)),
        ],
        out_shape=[
            jax.ShapeDtypeStruct((batch, t, LANES), F32),
            jax.ShapeDtypeStruct((batch, LANES, t), F32),
        ],
        compiler_params=_params(("parallel",)),
        name="gate_prep",
    )(g3)


def _scan_kernel(k_ref, o_ref, qt_ref, vt_ref, pcol_ref, prow_ref,
                 kc_ref, vtc_ref, prowc_ref, ng_ref,
                 y_ref, s_ref, min_ref, *, chunk, dk, dv):
    head = pl.program_id(1)
    t = k_ref.shape[1]
    tc = kc_ref.shape[1]
    nc = t // chunk
    ncc = tc // chunk
    lane = lax.broadcasted_iota(jnp.int32, (chunk, LANES), 1)
    row = lax.broadcasted_iota(jnp.int32, (chunk, chunk), 0)
    colm = lax.broadcasted_iota(jnp.int32, (chunk, chunk), 1)
    ones_t = jnp.ones((LANES, chunk), BF16)
    dot = functools.partial(jnp.dot, preferred_element_type=F32)

    def column(p, idx):
        return jnp.sum(jnp.where(lane == idx, p, 0.0), axis=1, keepdims=True)

    directions = ((0, 0, chunk - 1), (1, 2 * HEADS, 0))

    def gate_rows(pr_ref, t0, base):
        a_row = pr_ref[0, pl.ds(base + head, 1), pl.ds(t0, chunk)]
        c_row = pr_ref[0, pl.ds(base + HEADS + head, 1), pl.ds(t0, chunk)]
        return a_row, c_row

    def aug_t(vt):
        return jnp.concatenate([vt, ones_t], axis=0)

    def local_sum(k, vt, a_row, tot):
        w_end = tot + a_row
        mloc = jnp.max(w_end, axis=1, keepdims=True)
        w = jnp.exp(w_end - mloc)
        wv_t = jnp.concatenate([(vt.astype(F32) * w).astype(BF16),
                                jnp.broadcast_to(w, (LANES, chunk)).astype(BF16)], axis=0)
        return dot(wv_t, k), mloc

    def merge(caug, m, tot, g, mloc):
        m_new = jnp.maximum(tot + m, mloc)
        return jnp.exp(tot + m - m_new) * caug + jnp.exp(mloc - m_new) * g, m_new

    def scalar_tile(x):
        return jnp.broadcast_to(x, (8, LANES))

    states = []
    for d, base, last in directions:
        caug, m = jnp.zeros((dv + LANES, dk), F32), jnp.full((1, 1), M_INIT, F32)
        for ci in (range(ncc) if d == 0 else reversed(range(ncc))):
            t0 = ci * chunk
            a_row, c_row = gate_rows(prowc_ref, t0, base)
            tot = c_row[:, last:last + 1]
            g, mloc = local_sum(kc_ref[0, pl.ds(t0, chunk), :], vtc_ref[0, :, pl.ds(t0, chunk)], a_row, tot)
            caug, m = merge(caug, m, tot, g, mloc)
        states += [caug, m]

    def state_pass(step, carry):
        carry = list(carry)
        for d, base, last in directions:
            ci = step if d == 0 else nc - 1 - step
            t0 = pl.multiple_of(ci * chunk, chunk)
            caug, m = carry[2 * d], carry[2 * d + 1]
            s_ref[d, ci] = caug.astype(BF16)
            min_ref[d, ci] = scalar_tile(m)
            a_row, c_row = gate_rows(prow_ref, t0, base)
            tot = c_row[:, last:last + 1]
            g, mloc = local_sum(k_ref[0, pl.ds(t0, chunk), :], vt_ref[0, :, pl.ds(t0, chunk)], a_row, tot)
            carry[2 * d], carry[2 * d + 1] = merge(caug, m, tot, g, mloc)
        return tuple(carry)

    lax.fori_loop(0, nc, state_pass, tuple(states), unroll=SCAN_UNROLL_A)

    masks = (row <= colm, row >= colm)
    ng = ng_ref[...]

    def pass_c(ci, carry):
        t0 = pl.multiple_of(ci * chunk, chunk)
        sl = pl.ds(t0, chunk)
        qt = qt_ref[0, :, sl]
        vaug_t = aug_t(vt_ref[0, :, sl])
        s_t = dot(k_ref[0, sl, :], qt)
        pc = pcol_ref[0, sl, :]
        h_t = None
        for d, base, last in directions:
            a_col = column(pc, base + head)
            _, c_row = gate_rows(prow_ref, t0, base)
            run_row = prow_ref[0, pl.ds(RUNMAX_LANE + base + head, 1), sl]
            inter = c_row + min_ref[d, ci][0:1, 0:1]
            m_out = jnp.maximum(inter, c_row + run_row)
            p_t = (s_t * jnp.exp(jnp.where(masks[d], a_col + (c_row - m_out), -jnp.inf))).astype(BF16)
            qs = (qt.astype(F32) * jnp.exp(inter - m_out)).astype(BF16)
            r_t = dot(vaug_t, p_t) + dot(s_ref[d, ci], qs)
            inv = 1.0 / jnp.maximum(jnp.abs(r_t[dv:dv + 1]), jnp.exp(-m_out))
            hd = r_t[:dv] * inv
            h_t = hd if h_t is None else h_t + hd
        hn_t = h_t * lax.rsqrt(jnp.mean(h_t * h_t, axis=0, keepdims=True) + RMS_EPS)
        y_ref[0, sl, :] = (hn_t.T * ng * _sigmoid(o_ref[0, sl, :].astype(F32))).astype(BF16)
        return carry

    lax.fori_loop(0, nc, pass_c, 0, unroll=SCAN_UNROLL_C)


def _scan(main, tr, pcol, prow, main_c, tr_c, prowc, norm_g, batch, t, tc, d):
    dv = d // HEADS
    dk = dv // 2
    wide0 = HEADS * dk // dv
    main3 = main.reshape(batch, t, main.shape[1])
    main_c3 = main_c.reshape(batch, tc, main_c.shape[1])
    nc = t // SCAN_L
    return pl.pallas_call(
        functools.partial(_scan_kernel, chunk=SCAN_L, dk=dk, dv=dv),
        grid=(batch, HEADS),
        in_specs=[
            pl.BlockSpec((1, t, dk), lambda b, h: (b, 0, h)),
            pl.BlockSpec((1, t, dv), lambda b, h: (b, 0, wide0 + h)),
            pl.BlockSpec((1, dk, t), lambda b, h: (b, h, 0)),
            pl.BlockSpec((1, dv, t), lambda b, h: (b, wide0 + h, 0)),
            pl.BlockSpec((1, t, LANES), lambda b, h: (b, 0, 0)),
            pl.BlockSpec((1, LANES, t), lambda b, h: (b, 0, 0)),
            pl.BlockSpec((1, tc, dk), lambda b, h: (b, 0, h)),
            pl.BlockSpec((1, dv, tc), lambda b, h: (b, h, 0)),
            pl.BlockSpec((1, LANES, tc), lambda b, h: (b, 0, 0)),
            pl.BlockSpec((1, dv), lambda b, h: (0, h)),
        ],
        out_specs=pl.BlockSpec((1, t, dv), lambda b, h: (b, 0, h)),
        out_shape=jax.ShapeDtypeStruct((batch, t, d), BF16),
        scratch_shapes=[
            pltpu.VMEM((2, nc, dv + LANES, dk), BF16),
            pltpu.VMEM((2, nc, 8, LANES), F32),
        ],
        compiler_params=_params(("parallel", "arbitrary")),
        name="mlstm_scan",
    )(main3, main3, tr, tr, pcol, prow, main_c3, tr_c, prowc, norm_g.reshape(1, d))


def _outproj_kernel(y_ref, x_ref, mod_ref, w_ref, o_ref):
    gate = mod_ref[0][2:3]
    o_ref[...] = x_ref[...] + gate * jnp.dot(y_ref[...], w_ref[...], preferred_element_type=F32)


def _outproj(y, x, mod3, w, rows_per_mod, tm):
    m, d = x.shape
    tpm = rows_per_mod // tm
    return pl.pallas_call(
        _outproj_kernel,
        grid=(m // tm,),
        in_specs=[
            pl.BlockSpec((tm, d), lambda i: (i, 0)),
            pl.BlockSpec((tm, d), lambda i: (i, 0)),
            pl.BlockSpec((1, 3, d), lambda i: (i // tpm, 0, 0)),
            pl.BlockSpec((d, d), lambda i: (0, 0)),
        ],
        out_specs=pl.BlockSpec((tm, d), lambda i: (i, 0)),
        out_shape=jax.ShapeDtypeStruct((m, d), F32),
        compiler_params=_params(("parallel",)),
        name="mlstm_out",
    )(y, x, mod3, w)


def _conv_kernel(x_ref, g_ref, mod_ref, wb_ref, wc_ref, wu_ref, cw_ref, wo_ref, o_ref, xn_ref, *, tn):
    j = pl.program_id(1)
    nj = pl.num_programs(1)
    mod = mod_ref[0]
    shift, gate = mod[0:1], mod[2:3]
    gain = g_ref[...] * (1.0 + mod[1:2])
    d = o_ref.shape[1]
    rg = x_ref.shape[0] // ROW_GROUPS
    cw = cw_ref[...]
    pos = lax.broadcasted_iota(jnp.int32, (rg, tn), 0) % GRID_W

    def step(first, last):
        acts = []
        for r in range(ROW_GROUPS):
            rows = slice(r * rg, (r + 1) * rg)
            if first:
                xn = _norm_rows(x_ref, r * rg, rg, gain, shift)
                xn_ref[rows, :] = xn
            else:
                xn = xn_ref[rows, :]
            bg = jnp.dot(xn, wb_ref[...], preferred_element_type=F32)
            cg = jnp.dot(xn, wc_ref[...], preferred_element_type=F32)
            u = jnp.dot(xn, wu_ref[...], preferred_element_type=F32)
            z = cg * u
            z_prev = jnp.where(pos == 0, 0.0, pltpu.roll(z, 1, 0))
            z_next = jnp.where(pos == GRID_W - 1, 0.0, pltpu.roll(z, rg - 1, 0))
            zc = cw[0:1] * z_prev + cw[1:2] * z + cw[2:3] * z_next
            acts.append((bg * zc).astype(BF16))
        for r in range(ROW_GROUPS):
            rows = slice(r * rg, (r + 1) * rg)
            for n0 in range(0, d, OUT_TN):
                cols = slice(n0, n0 + OUT_TN)
                upd = jnp.dot(acts[r], wo_ref[:, cols], preferred_element_type=F32)
                acc = upd if first else o_ref[rows, cols] + upd
                if last:
                    acc = x_ref[rows, cols] + gate[:, cols] * acc
                o_ref[rows, cols] = acc

    pl.when(j == 0)(lambda: step(True, False))
    pl.when((j > 0) & (j < nj - 1))(lambda: step(False, False))
    pl.when(j == nj - 1)(lambda: step(False, True))


def _conv(x, g, mod3, w3, cw, wo, rows_per_mod, tm):
    m, d = x.shape
    nj = d // CONV_TN
    tpm = rows_per_mod // tm
    return pl.pallas_call(
        functools.partial(_conv_kernel, tn=CONV_TN),
        grid=(m // tm, nj),
        in_specs=[
            pl.BlockSpec((tm, d), lambda i, j: (i, 0)),
            pl.BlockSpec((1, d), lambda i, j: (0, 0)),
            pl.BlockSpec((1, 3, d), lambda i, j: (i // tpm, 0, 0)),
            pl.BlockSpec((d, CONV_TN), lambda i, j: (0, j)),
            pl.BlockSpec((d, CONV_TN), lambda i, j: (0, nj + j)),
            pl.BlockSpec((d, CONV_TN), lambda i, j: (0, 2 * nj + j)),
            pl.BlockSpec((3, CONV_TN), lambda i, j: (0, j)),
            pl.BlockSpec((CONV_TN, d), lambda i, j: (j, 0)),
        ],
        out_specs=pl.BlockSpec((tm, d), lambda i, j: (i, 0)),
        out_shape=jax.ShapeDtypeStruct((m, d), F32),
        scratch_shapes=[pltpu.VMEM((tm, d), BF16)],
        compiler_params=_params(("parallel", "arbitrary")),
        name="conv_mixer",
    )(x, g.reshape(1, d), mod3, w3, w3, w3, cw, wo)


def kernel(x, c, ctx, c_ctx, w_mod, b_mod, norm_g, ffn_w_in, ffn_w_out, mlstm_w_in, mlstm_b_gate,
           mlstm_norm_g, mlstm_w_out, conv_w_in, conv_w, conv_w_out, final_norm_g):
    batch, t, d = x.shape
    tc = ctx.shape[1]
    depth = w_mod.shape[0]
    assert depth == 2 and d % (2 * HEADS * LANES) == 0 and t % GRID_W == 0
    dv = d // HEADS
    dk = dv // 2
    qk = HEADS * dk

    c_rows = jnp.concatenate([c, c_ctx[None, :], jnp.zeros((-(batch + 1) % 8, d), F32)], axis=0)
    mod = _modulation(c_rows, w_mod, b_mod).reshape(depth, c_rows.shape[0], N_MOD, d)

    def mod3(layer, sub, context=False):
        rows = mod[layer, batch:batch + 1] if context else mod[layer, :batch]
        return rows[:, 3 * sub:3 * sub + 3, :]

    tm = min(ROW_TM, t)
    tmix = min(MIX_TM, t)
    xf = x.reshape(batch * t, d)
    cf = ctx.reshape(batch * tc, d)

    ffn_in = ffn_w_in.astype(BF16)
    ffn_out = ffn_w_out.astype(BF16)
    xf = _ffn(xf, norm_g[0, 0], mod3(0, 0), ffn_in, ffn_out, 0, 0, t, tm)
    cf = _ffn(cf, norm_g[0, 0], mod3(0, 0, True), ffn_in, ffn_out, 0, 0, batch * tc, tm)

    w_in = mlstm_w_in[0]
    w_qt = (w_in[:, :qk] * (dk ** -0.5)).astype(BF16).T
    w_k = w_in[:, qk:2 * qk].astype(BF16)
    w_vt = w_in[:, 2 * qk:2 * qk + d].astype(BF16).T
    g0 = 2 * qk + d
    ng = 4 * HEADS
    w_gate = jnp.pad(w_in[:, g0:g0 + ng].astype(BF16), ((0, 0), (0, LANES - ng)))
    b_gate = jnp.pad(mlstm_b_gate[0], (0, LANES - ng)).reshape(1, LANES)
    w_o = w_in[:, g0 + ng:].astype(BF16)
    main, tr, gates = _proj(xf, norm_g[0, 1], mod3(0, 1), jnp.concatenate([w_k, w_o], axis=1),
                            jnp.concatenate([w_qt, w_vt], axis=0), w_gate, b_gate, t, min(PROJ_TM, t))
    main_c, tr_c, gates_c = _proj(cf, norm_g[0, 1], mod3(0, 1, True), w_k, w_vt, w_gate, b_gate,
                                  tc, min(PROJ_TM, tc))
    pcol, prow = _gateprep(gates, batch, SCAN_L)
    _, prowc = _gateprep(gates_c, batch, SCAN_L)
    y = _scan(main, tr, pcol, prow, main_c, tr_c, prowc, mlstm_norm_g[0], batch, t, tc, d)
    xf = _outproj(y.reshape(batch * t, d), xf, mod3(0, 1), mlstm_w_out[0].astype(BF16), t, tmix)

    xf = _ffn(xf, norm_g[0, 2], mod3(0, 2), ffn_in, ffn_out, 0, 1, t, tm)

    xf = _ffn(xf, norm_g[1, 0], mod3(1, 0), ffn_in, ffn_out, 1, 0, t, tm)
    xf = _conv(xf, norm_g[1, 1], mod3(1, 1), conv_w_in[0].astype(BF16), conv_w[0],
               conv_w_out[0].astype(BF16), t, tmix)
    xf = _ffn(xf, norm_g[1, 2], mod3(1, 2), ffn_in, ffn_out, 1, 1, t, tm,
              final_g=final_norm_g)
    return xf.reshape(batch, t, d)
```

```python
import functools

import jax
import jax.numpy as jnp
from jax import lax
from jax.experimental import pallas as pl
from jax.experimental.pallas import tpu as pltpu

F32 = jnp.float32
BF16 = jnp.bfloat16

HEADS = 8
N_MOD = 9
GRID_W = 64
RMS_EPS = 1e-6
M_INIT = -1e30

LANES = 128
MXU_COLS = 256
VMEM_LIMIT_BYTES = 60 * 1024 * 1024

NORM_ROWS = 64
EPI_ROWS = 256
ROW_TM = 1024
MIX_TM = 512
ROW_GROUPS = 4
OUT_TN = 512
FFN_TF = 512
CONV_TN = 512
PROJ_TN = 1024
PROJ_TM = 1024
SCAN_L = 256
RUNMAX_LANE = 4 * HEADS
SCAN_UNROLL_A = 4
SCAN_UNROLL_C = 4
MOD_TN = 1024


def _params(sem):
    return pltpu.CompilerParams(dimension_semantics=sem, vmem_limit_bytes=VMEM_LIMIT_BYTES)


def _sigmoid(x):
    return 1.0 / (1.0 + jnp.exp(-x))


def _norm_rows(x_ref, start, rows, gain, shift):
    rc = min(NORM_ROWS, rows)
    out = []
    for r0 in range(start, start + rows, rc):
        x = x_ref[r0:r0 + rc, :]
        inv = lax.rsqrt(jnp.mean(x * x, axis=-1, keepdims=True) + RMS_EPS)
        out.append((x_ref[r0:r0 + rc, :] * inv * gain + shift).astype(BF16))
    return jnp.concatenate(out, axis=0)


def _fill_xn(x_ref, xn_ref, g, mod):
    rows = x_ref.shape[0]
    rc = min(NORM_ROWS, rows)
    shift = mod[0:1]
    gain = g * (1.0 + mod[1:2])

    n = rows // rc

    def inv_rms(r):
        x = x_ref[pl.ds(pl.multiple_of(r * rc, rc), rc), :]
        return lax.rsqrt(jnp.mean(x * x, axis=-1, keepdims=True) + RMS_EPS)

    def body(r, inv):
        inv_next = inv_rms(jnp.minimum(r + 1, n - 1))
        sl = pl.ds(pl.multiple_of(r * rc, rc), rc)
        xn_ref[sl, :] = (x_ref[sl, :] * inv * gain + shift).astype(BF16)
        return inv_next

    lax.fori_loop(0, n, body, inv_rms(0))


def _mod_kernel(c_ref, w_ref, b_ref, o_ref):
    c = c_ref[...]
    s = (c * _sigmoid(c)).astype(BF16)
    w = w_ref[0].astype(BF16)
    o_ref[0] = jnp.dot(s, w, preferred_element_type=F32) + b_ref[0]


def _modulation(c_rows, w_mod, b_mod):
    depth, d, n = w_mod.shape
    r = c_rows.shape[0]
    return pl.pallas_call(
        _mod_kernel,
        grid=(depth, n // MOD_TN),
        in_specs=[
            pl.BlockSpec((r, d), lambda l, j: (0, 0)),
            pl.BlockSpec((1, d, MOD_TN), lambda l, j: (l, 0, j)),
            pl.BlockSpec((1, 1, MOD_TN), lambda l, j: (l, 0, j)),
        ],
        out_specs=pl.BlockSpec((1, r, MOD_TN), lambda l, j: (l, 0, j)),
        out_shape=jax.ShapeDtypeStruct((depth, r, n), F32),
        compiler_params=_params(("arbitrary", "arbitrary")),
        name="modulation",
    )(c_rows, w_mod, b_mod.reshape(depth, 1, n))


def _ffn_tile_start(j, f):
    assert f % LANES == 0 and f >= FFN_TF
    return pl.multiple_of(jnp.minimum(j * FFN_TF, f - FFN_TF), LANES)


def _ffn_kernel(*refs, f, final):
    if final:
        x_ref, g_ref, mod_ref, wg_ref, wu_ref, wo_ref, fg_ref, o_ref, xn_ref = refs
    else:
        x_ref, g_ref, mod_ref, wg_ref, wu_ref, wo_ref, o_ref, xn_ref = refs
    j = pl.program_id(1)
    nj = pl.num_programs(1)
    mod = mod_ref[0]
    shift, half_gate = mod[0:1], 0.5 * mod[2:3]
    gain = g_ref[...] * (1.0 + mod[1:2])
    d = o_ref.shape[1]
    rg = x_ref.shape[0] // ROW_GROUPS
    col = lax.broadcasted_iota(jnp.int32, (rg, FFN_TF), 1)

    def step(first, last):
        done = j * FFN_TF - _ffn_tile_start(j, f)
        acts = []
        for r in range(ROW_GROUPS):
            rows = slice(r * rg, (r + 1) * rg)
            if first:
                xn = _norm_rows(x_ref, r * rg, rg, gain, shift)
                xn_ref[rows, :] = xn
            else:
                xn = xn_ref[rows, :]
            hg = jnp.dot(xn, wg_ref[...], preferred_element_type=F32)
            hu = jnp.dot(xn, wu_ref[...], preferred_element_type=F32)
            acts.append((hg * _sigmoid(hg) * jnp.where(col >= done, hu, 0.0)).astype(BF16))
        for r in range(ROW_GROUPS):
            rows = slice(r * rg, (r + 1) * rg)
            for n0 in range(0, d, OUT_TN):
                cols = slice(n0, n0 + OUT_TN)
                upd = jnp.dot(acts[r], wo_ref[:, cols], preferred_element_type=F32)
                acc = upd if first else o_ref[rows, cols] + upd
                if last:
                    acc = x_ref[rows, cols] + half_gate[:, cols] * acc
                o_ref[rows, cols] = acc
            if last and final:
                y = o_ref[rows, :]
                ms = jnp.mean(y * y, axis=-1, keepdims=True)
                o_ref[rows, :] = y * lax.rsqrt(ms + RMS_EPS) * fg_ref[...]

    pl.when(j == 0)(lambda: step(True, False))
    pl.when((j > 0) & (j < nj - 1))(lambda: step(False, False))
    pl.when(j == nj - 1)(lambda: step(False, True))


def _ffn(x, g, mod3, w_in, w_out, rows_per_mod, tm, final_g=None):
    m, d = x.shape
    f = w_out.shape[0]
    nj = -(-f // FFN_TF)
    tpm = rows_per_mod // tm
    final = final_g is not None
    in_specs = [
        pl.BlockSpec((tm, d), lambda i, j: (i, 0)),
        pl.BlockSpec((1, d), lambda i, j: (0, 0)),
        pl.BlockSpec((1, 3, d), lambda i, j: (i // tpm, 0, 0)),
        pl.BlockSpec((pl.Element(d), pl.Element(FFN_TF)), lambda i, j: (0, _ffn_tile_start(j, f))),
        pl.BlockSpec((pl.Element(d), pl.Element(FFN_TF)),
                     lambda i, j: (0, pl.multiple_of(f + _ffn_tile_start(j, f), LANES))),
        pl.BlockSpec((pl.Element(FFN_TF), pl.Element(d)), lambda i, j: (_ffn_tile_start(j, f), 0)),
    ]
    args = [x, g.reshape(1, d), mod3, w_in, w_in, w_out]
    if final:
        in_specs.append(pl.BlockSpec((1, d), lambda i, j: (0, 0)))
        args.append(final_g.reshape(1, d))
    return pl.pallas_call(
        functools.partial(_ffn_kernel, f=f, final=final),
        grid=(m // tm, nj),
        in_specs=in_specs,
        out_specs=pl.BlockSpec((tm, d), lambda i, j: (i, 0)),
        out_shape=jax.ShapeDtypeStruct((m, d), F32),
        scratch_shapes=[pltpu.VMEM((tm, d), BF16)],
        compiler_params=_params(("parallel", "arbitrary")),
        name="ffn_final" if final else "ffn",
    )(*args)


def _cast_plan(w_in, w_out, layer, sub, nsteps, step):
    d, f2 = w_in.shape[2:]
    f = w_out.shape[2]
    rb = d // nsteps
    cw = max(LANES, d // nsteps)
    per = nsteps // (d // cw)
    assert rb * nsteps == d and rb % 16 == 0 and per * (d // cw) == nsteps
    in_specs = [
        pl.BlockSpec((None, None, rb, f2), lambda *gi: (layer, sub, step(*gi), 0)),
        pl.BlockSpec((None, None, f, cw), lambda *gi: (layer, sub, 0, step(*gi) // per)),
    ]
    out_specs = [
        pl.BlockSpec((rb, f2), lambda *gi: (step(*gi), 0)),
        pl.BlockSpec((f, cw), lambda *gi: (0, step(*gi) // per)),
    ]
    out_shape = [jax.ShapeDtypeStruct((d, f2), BF16), jax.ShapeDtypeStruct((f, d), BF16)]
    return in_specs, out_specs, out_shape


def _cast_blocks(src_in_ref, src_out_ref, dst_in_ref, dst_out_ref):
    dst_in_ref[...] = src_in_ref[...].astype(BF16)
    dst_out_ref[...] = src_out_ref[...].astype(BF16)


def _proj_kernel(x_ref, g_ref, mod_ref, w_ref, wt_ref, wg_ref, bg_ref,
                 main_ref, tr_ref, gates_ref, xn_ref, *, n_main, n_tr):
    j = pl.program_id(1)
    mod = mod_ref[0]
    rg = x_ref.shape[0] // ROW_GROUPS

    @pl.when(j == 0)
    def _():
        gain = g_ref[...] * (1.0 + mod[1:2])
        for r in range(ROW_GROUPS):
            rows = slice(r * rg, (r + 1) * rg)
            xn = _norm_rows(x_ref, r * rg, rg, gain, mod[0:1])
            xn_ref[rows, :] = xn
            gates_ref[rows, :] = jnp.dot(xn, wg_ref[...], preferred_element_type=F32) + bg_ref[...]
            main_ref[rows, :] = jnp.dot(xn, w_ref[...], preferred_element_type=F32).astype(BF16)
            tr = lax.dot_general(wt_ref[...], xn, (((1,), (1,)), ((), ())), preferred_element_type=F32)
            tr_ref[0, :, rows] = tr.astype(BF16)

    def main_tile():
        main_ref[...] = jnp.dot(xn_ref[...], w_ref[...], preferred_element_type=F32).astype(BF16)

    def tr_tile():
        tr = lax.dot_general(wt_ref[...], xn_ref[...], (((1,), (1,)), ((), ())), preferred_element_type=F32)
        tr_ref[0] = tr.astype(BF16)

    def both_tiles():
        main_tile()
        tr_tile()

    both = min(n_main, n_tr)
    if both > 1:
        pl.when((j > 0) & (j < both))(both_tiles)
    if n_main != n_tr:
        pl.when(j >= both)(main_tile if n_main > n_tr else tr_tile)


def _proj(x, g, mod3, w_main, w_tr, w_gate, b_gate, rows_per_batch, tm):
    m, d = x.shape
    n = w_main.shape[1]
    nt = w_tr.shape[0]
    n_main, n_tr = n // PROJ_TN, nt // PROJ_TN
    batch = m // rows_per_batch
    tpb = rows_per_batch // tm
    nmod = mod3.shape[0]
    mod_idx = (lambda i, j: (i // tpb, 0, 0)) if nmod > 1 else (lambda i, j: (0, 0, 0))
    return pl.pallas_call(
        functools.partial(_proj_kernel, n_main=n_main, n_tr=n_tr),
        grid=(m // tm, max(n_main, n_tr)),
        in_specs=[
            pl.BlockSpec((tm, d), lambda i, j: (i, 0)),
            pl.BlockSpec((1, d), lambda i, j: (0, 0)),
            pl.BlockSpec((1, 3, d), mod_idx),
            pl.BlockSpec((d, PROJ_TN), lambda i, j: (0, jnp.minimum(j, n_main - 1))),
            pl.BlockSpec((PROJ_TN, d), lambda i, j: (jnp.minimum(j, n_tr - 1), 0)),
            pl.BlockSpec((d, LANES), lambda i, j: (0, 0)),
            pl.BlockSpec((1, LANES), lambda i, j: (0, 0)),
        ],
        out_specs=[
            pl.BlockSpec((tm, PROJ_TN), lambda i, j: (i, jnp.minimum(j, n_main - 1))),
            pl.BlockSpec((1, PROJ_TN, tm), lambda i, j: (i // tpb, jnp.minimum(j, n_tr - 1), i % tpb)),
            pl.BlockSpec((tm, LANES), lambda i, j: (i, 0)),
        ],
        out_shape=[
            jax.ShapeDtypeStruct((m, n), BF16),
            jax.ShapeDtypeStruct((batch, nt, rows_per_batch), BF16),
            jax.ShapeDtypeStruct((m, LANES), F32),
        ],
        scratch_shapes=[pltpu.VMEM((tm, d), BF16)],
        compiler_params=_params(("parallel", "arbitrary")),
        name="mlstm_proj",
    )(x, g.reshape(1, d), mod3, w_main, w_tr, w_gate, b_gate)


def _tri_sum(tri, x):
    x1 = x.astype(BF16)
    r1 = x - x1.astype(F32)
    x2 = r1.astype(BF16)
    x3 = (r1 - x2.astype(F32)).astype(BF16)
    dot = functools.partial(jnp.dot, preferred_element_type=F32)
    return dot(tri, x1) + dot(tri, x2) + dot(tri, x3)


def _gateprep_kernel(g_ref, pcol_ref, prow_ref, *, chunk):
    t = g_ref.shape[1]
    row = lax.broadcasted_iota(jnp.int32, (chunk, chunk), 0)
    col = lax.broadcasted_iota(jnp.int32, (chunk, chunk), 1)
    tril = jnp.where(row >= col, 1.0, 0.0).astype(BF16)
    triu = jnp.where(row <= col, 1.0, 0.0).astype(BF16)
    lane = lax.broadcasted_iota(jnp.int32, (chunk, LANES), 1)
    f_fwd = (lane >= HEADS) & (lane < 2 * HEADS)
    f_bwd = (lane >= 3 * HEADS) & (lane < 4 * HEADS)
    tok = lax.broadcasted_iota(jnp.int32, (chunk, LANES), 0)
    run_lanes = ((lane >= RUNMAX_LANE) & (lane < RUNMAX_LANE + HEADS)) | (
        (lane >= RUNMAX_LANE + 2 * HEADS) & (lane < RUNMAX_LANE + 3 * HEADS))

    def body(ci, carry):
        sl = pl.ds(pl.multiple_of(ci * chunk, chunk), chunk)
        g = g_ref[0, sl, :]
        logf = jnp.minimum(g, 0.0) - jnp.log(1.0 + jnp.exp(-jnp.abs(g)))
        logf = jnp.where(f_fwd | f_bwd, logf, 0.0)
        cum = jnp.where(f_fwd, _tri_sum(tril, logf), jnp.where(f_bwd, _tri_sum(triu, logf), 0.0))
        p = jnp.where(f_fwd | f_bwd, cum, g - pltpu.roll(cum, LANES - HEADS, 1))
        pre, suf = p, p
        for sh in [1 << b for b in range(chunk.bit_length() - 1)]:
            pre = jnp.maximum(pre, jnp.where(tok >= sh, pltpu.roll(pre, sh, 0), -jnp.inf))
            suf = jnp.maximum(suf, jnp.where(tok < chunk - sh, pltpu.roll(suf, chunk - sh, 0), -jnp.inf))
        run = jnp.where(lane < 2 * HEADS, pre, suf)
        p = jnp.where(run_lanes, pltpu.roll(run, RUNMAX_LANE, 1), p)
        pcol_ref[0, sl, :] = p
        prow_ref[0, :, sl] = p.T
        return carry

    lax.fori_loop(0, t // chunk, body, 0)


def _gateprep(gates, batch, chunk):
    m = gates.shape[0]
    t = m // batch
    g3 = gates.reshape(batch, t, LANES)
    return pl.pallas_call(
        functools.partial(_gateprep_kernel, chunk=chunk),
        grid=(batch,),
        in_specs=[pl.BlockSpec((1, t, LANES), lambda b: (b, 0, 0))],
        out_specs=[
            pl.BlockSpec((1, t, LANES), lambda b: (b, 0, 0)),
            pl.BlockSpec((1, LANES, t), lambda b: (b, 0, 0)),
        ],
        out_shape=[
            jax.ShapeDtypeStruct((batch, t, LANES), F32),
            jax.ShapeDtypeStruct((batch, LANES, t), F32),
        ],
        compiler_params=_params(("parallel",)),
        name="gate_prep",
    )(g3)


def _scan_kernel(k_ref, o_ref, qt_ref, vt_ref, pcol_ref, prow_ref,
                 kc_ref, vtc_ref, prowc_ref, ng_ref, cast_in_ref, cast_out_ref,
                 y_ref, w_in_ref, w_out_ref, s_ref, min_ref, *, chunk, dk, dv):
    _cast_blocks(cast_in_ref, cast_out_ref, w_in_ref, w_out_ref)
    head = pl.program_id(1)
    t = k_ref.shape[1]
    tc = kc_ref.shape[1]
    nc = t // chunk
    ncc = tc // chunk
    lane = lax.broadcasted_iota(jnp.int32, (chunk, LANES), 1)
    row = lax.broadcasted_iota(jnp.int32, (chunk, chunk), 0)
    colm = lax.broadcasted_iota(jnp.int32, (chunk, chunk), 1)
    ones_t = jnp.ones((LANES, chunk), BF16)
    dot = functools.partial(jnp.dot, preferred_element_type=F32)

    def column(p, idx):
        return jnp.sum(jnp.where(lane == idx, p, 0.0), axis=1, keepdims=True)

    directions = ((0, 0, chunk - 1), (1, 2 * HEADS, 0))

    def gate_rows(pr_ref, t0, base):
        a_row = pr_ref[0, pl.ds(base + head, 1), pl.ds(t0, chunk)]
        c_row = pr_ref[0, pl.ds(base + HEADS + head, 1), pl.ds(t0, chunk)]
        return a_row, c_row

    def aug_t(vt):
        return jnp.concatenate([vt, ones_t], axis=0)

    def local_sum(k, vt, a_row, tot):
        w_end = tot + a_row
        mloc = jnp.max(w_end, axis=1, keepdims=True)
        w = jnp.exp(w_end - mloc)
        wv_t = jnp.concatenate([(vt.astype(F32) * w).astype(BF16),
                                jnp.broadcast_to(w, (LANES, chunk)).astype(BF16)], axis=0)
        return dot(wv_t, k), mloc

    def merge(caug, m, tot, g, mloc):
        m_new = jnp.maximum(tot + m, mloc)
        return jnp.exp(tot + m - m_new) * caug + jnp.exp(mloc - m_new) * g, m_new

    def scalar_tile(x):
        return jnp.broadcast_to(x, (8, LANES))

    states = []
    for d, base, last in directions:
        caug, m = jnp.zeros((dv + LANES, dk), F32), jnp.full((1, 1), M_INIT, F32)
        for ci in (range(ncc) if d == 0 else reversed(range(ncc))):
            t0 = ci * chunk
            a_row, c_row = gate_rows(prowc_ref, t0, base)
            tot = c_row[:, last:last + 1]
            g, mloc = local_sum(kc_ref[0, pl.ds(t0, chunk), :], vtc_ref[0, :, pl.ds(t0, chunk)], a_row, tot)
            caug, m = merge(caug, m, tot, g, mloc)
        states += [caug, m]

    def state_pass(step, carry):
        carry = list(carry)
        for d, base, last in directions:
            ci = step if d == 0 else nc - 1 - step
            t0 = pl.multiple_of(ci * chunk, chunk)
            caug, m = carry[2 * d], carry[2 * d + 1]
            s_ref[d, ci] = caug.astype(BF16)
            min_ref[d, ci] = scalar_tile(m)
            a_row, c_row = gate_rows(prow_ref, t0, base)
            tot = c_row[:, last:last + 1]
            g, mloc = local_sum(k_ref[0, pl.ds(t0, chunk), :], vt_ref[0, :, pl.ds(t0, chunk)], a_row, tot)
            carry[2 * d], carry[2 * d + 1] = merge(caug, m, tot, g, mloc)
        return tuple(carry)

    lax.fori_loop(0, nc, state_pass, tuple(states), unroll=SCAN_UNROLL_A)

    masks = (row <= colm, row >= colm)
    ng = ng_ref[...]

    def pass_c(ci, carry):
        t0 = pl.multiple_of(ci * chunk, chunk)
        sl = pl.ds(t0, chunk)
        qt = qt_ref[0, :, sl]
        vaug_t = aug_t(vt_ref[0, :, sl])
        s_t = dot(k_ref[0, sl, :], qt)
        pc = pcol_ref[0, sl, :]
        h_t = None
        for d, base, last in directions:
            a_col = column(pc, base + head)
            _, c_row = gate_rows(prow_ref, t0, base)
            run_row = prow_ref[0, pl.ds(RUNMAX_LANE + base + head, 1), sl]
            inter = c_row + min_ref[d, ci][0:1, 0:1]
            m_out = jnp.maximum(inter, c_row + run_row)
            p_t = (s_t * jnp.exp(jnp.where(masks[d], a_col + (c_row - m_out), -jnp.inf))).astype(BF16)
            qs = (qt.astype(F32) * jnp.exp(inter - m_out)).astype(BF16)
            r_t = dot(vaug_t, p_t) + dot(s_ref[d, ci], qs)
            inv = 1.0 / jnp.maximum(jnp.abs(r_t[dv:dv + 1]), jnp.exp(-m_out))
            hd = r_t[:dv] * inv
            h_t = hd if h_t is None else h_t + hd
        hn_t = h_t * lax.rsqrt(jnp.mean(h_t * h_t, axis=0, keepdims=True) + RMS_EPS)
        y_ref[0, sl, :] = (hn_t.T * ng * _sigmoid(o_ref[0, sl, :].astype(F32))).astype(BF16)
        return carry

    lax.fori_loop(0, nc, pass_c, 0, unroll=SCAN_UNROLL_C)


def _scan(main, tr, pcol, prow, main_c, tr_c, prowc, norm_g, batch, t, tc, d, cast):
    dv = d // HEADS
    dk = dv // 2
    wide0 = HEADS * dk // dv
    main3 = main.reshape(batch, t, main.shape[1])
    main_c3 = main_c.reshape(batch, tc, main_c.shape[1])
    nc = t // SCAN_L
    cast_in, cast_out, cast_shape = _cast_plan(*cast, batch * HEADS, lambda b, h: b * HEADS + h)
    return pl.pallas_call(
        functools.partial(_scan_kernel, chunk=SCAN_L, dk=dk, dv=dv),
        grid=(batch, HEADS),
        in_specs=[
            pl.BlockSpec((1, t, dk), lambda b, h: (b, 0, h)),
            pl.BlockSpec((1, t, dv), lambda b, h: (b, 0, wide0 + h)),
            pl.BlockSpec((1, dk, t), lambda b, h: (b, h, 0)),
            pl.BlockSpec((1, dv, t), lambda b, h: (b, wide0 + h, 0)),
            pl.BlockSpec((1, t, LANES), lambda b, h: (b, 0, 0)),
            pl.BlockSpec((1, LANES, t), lambda b, h: (b, 0, 0)),
            pl.BlockSpec((1, tc, dk), lambda b, h: (b, 0, h)),
            pl.BlockSpec((1, dv, tc), lambda b, h: (b, h, 0)),
            pl.BlockSpec((1, LANES, tc), lambda b, h: (b, 0, 0)),
            pl.BlockSpec((1, dv), lambda b, h: (0, h)),
        ] + cast_in,
        out_specs=[pl.BlockSpec((1, t, dv), lambda b, h: (b, 0, h))] + cast_out,
        out_shape=[jax.ShapeDtypeStruct((batch, t, d), BF16)] + cast_shape,
        scratch_shapes=[
            pltpu.VMEM((2, nc, dv + LANES, dk), BF16),
            pltpu.VMEM((2, nc, 8, LANES), F32),
        ],
        compiler_params=_params(("arbitrary", "arbitrary")),
        name="mlstm_scan",
    )(main3, main3, tr, tr, pcol, prow, main_c3, tr_c, prowc, norm_g.reshape(1, d), cast[0], cast[1])


def _outproj_kernel(y_ref, x_ref, mod_ref, w_ref, cast_in_ref, cast_out_ref, o_ref, w_in_ref, w_out_ref):
    _cast_blocks(cast_in_ref, cast_out_ref, w_in_ref, w_out_ref)
    gate = mod_ref[0][2:3]
    o_ref[...] = x_ref[...] + gate * jnp.dot(y_ref[...], w_ref[...], preferred_element_type=F32)


def _outproj(y, x, mod3, w, rows_per_mod, tm, cast):
    m, d = x.shape
    tpm = rows_per_mod // tm
    cast_in, cast_out, cast_shape = _cast_plan(*cast, m // tm, lambda i: i)
    return pl.pallas_call(
        _outproj_kernel,
        grid=(m // tm,),
        in_specs=[
            pl.BlockSpec((tm, d), lambda i: (i, 0)),
            pl.BlockSpec((tm, d), lambda i: (i, 0)),
            pl.BlockSpec((1, 3, d), lambda i: (i // tpm, 0, 0)),
            pl.BlockSpec((d, d), lambda i: (0, 0)),
        ] + cast_in,
        out_specs=[pl.BlockSpec((tm, d), lambda i: (i, 0))] + cast_out,
        out_shape=[jax.ShapeDtypeStruct((m, d), F32)] + cast_shape,
        compiler_params=_params(("arbitrary",)),
        name="mlstm_out",
    )(y, x, mod3, w, cast[0], cast[1])


def _conv_kernel(x_ref, g_ref, mod_ref, wb_ref, wc_ref, wu_ref, cw_ref, wo_ref, o_ref, xn_ref, *, tn):
    j = pl.program_id(1)
    nj = pl.num_programs(1)
    mod = mod_ref[0]
    shift, gate = mod[0:1], mod[2:3]
    gain = g_ref[...] * (1.0 + mod[1:2])
    d = o_ref.shape[1]
    rg = x_ref.shape[0] // ROW_GROUPS
    cw = cw_ref[...]
    pos = lax.broadcasted_iota(jnp.int32, (rg, tn), 0) % GRID_W

    def step(first, last):
        acts = []
        for r in range(ROW_GROUPS):
            rows = slice(r * rg, (r + 1) * rg)
            if first:
                xn = _norm_rows(x_ref, r * rg, rg, gain, shift)
                xn_ref[rows, :] = xn
            else:
                xn = xn_ref[rows, :]
            bg = jnp.dot(xn, wb_ref[...], preferred_element_type=F32)
            cg = jnp.dot(xn, wc_ref[...], preferred_element_type=F32)
            u = jnp.dot(xn, wu_ref[...], preferred_element_type=F32)
            z = cg * u
            z_prev = jnp.where(pos == 0, 0.0, pltpu.roll(z, 1, 0))
            z_next = jnp.where(pos == GRID_W - 1, 0.0, pltpu.roll(z, rg - 1, 0))
            zc = cw[0:1] * z_prev + cw[1:2] * z + cw[2:3] * z_next
            acts.append((bg * zc).astype(BF16))
        for r in range(ROW_GROUPS):
            rows = slice(r * rg, (r + 1) * rg)
            for n0 in range(0, d, OUT_TN):
                cols = slice(n0, n0 + OUT_TN)
                upd = jnp.dot(acts[r], wo_ref[:, cols], preferred_element_type=F32)
                acc = upd if first else o_ref[rows, cols] + upd
                if last:
                    acc = x_ref[rows, cols] + gate[:, cols] * acc
                o_ref[rows, cols] = acc

    pl.when(j == 0)(lambda: step(True, False))
    pl.when((j > 0) & (j < nj - 1))(lambda: step(False, False))
    pl.when(j == nj - 1)(lambda: step(False, True))


def _conv(x, g, mod3, w3, cw, wo, rows_per_mod, tm):
    m, d = x.shape
    nj = d // CONV_TN
    tpm = rows_per_mod // tm
    return pl.pallas_call(
        functools.partial(_conv_kernel, tn=CONV_TN),
        grid=(m // tm, nj),
        in_specs=[
            pl.BlockSpec((tm, d), lambda i, j: (i, 0)),
            pl.BlockSpec((1, d), lambda i, j: (0, 0)),
            pl.BlockSpec((1, 3, d), lambda i, j: (i // tpm, 0, 0)),
            pl.BlockSpec((d, CONV_TN), lambda i, j: (0, j)),
            pl.BlockSpec((d, CONV_TN), lambda i, j: (0, nj + j)),
            pl.BlockSpec((d, CONV_TN), lambda i, j: (0, 2 * nj + j)),
            pl.BlockSpec((3, CONV_TN), lambda i, j: (0, j)),
            pl.BlockSpec((CONV_TN, d), lambda i, j: (j, 0)),
        ],
        out_specs=pl.BlockSpec((tm, d), lambda i, j: (i, 0)),
        out_shape=jax.ShapeDtypeStruct((m, d), F32),
        scratch_shapes=[pltpu.VMEM((tm, d), BF16)],
        compiler_params=_params(("parallel", "arbitrary")),
        name="conv_mixer",
    )(x, g.reshape(1, d), mod3, w3, w3, w3, cw, wo)


def kernel(x, c, ctx, c_ctx, w_mod, b_mod, norm_g, ffn_w_in, ffn_w_out, mlstm_w_in, mlstm_b_gate,
           mlstm_norm_g, mlstm_w_out, conv_w_in, conv_w, conv_w_out, final_norm_g):
    batch, t, d = x.shape
    tc = ctx.shape[1]
    depth = w_mod.shape[0]
    assert depth == 2 and d % (2 * HEADS * LANES) == 0 and t % GRID_W == 0
    dv = d // HEADS
    dk = dv // 2
    qk = HEADS * dk

    c_rows = jnp.concatenate([c, c_ctx[None, :], jnp.zeros((-(batch + 1) % 8, d), F32)], axis=0)
    mod = _modulation(c_rows, w_mod, b_mod).reshape(depth, c_rows.shape[0], N_MOD, d)

    def mod3(layer, sub, context=False):
        rows = mod[layer, batch:batch + 1] if context else mod[layer, :batch]
        return rows[:, 3 * sub:3 * sub + 3, :]

    tm = min(ROW_TM, t)
    tmix = min(MIX_TM, t)
    xf = x.reshape(batch * t, d)
    cf = ctx.reshape(batch * tc, d)

    w00_in, w00_out = ffn_w_in[0, 0].astype(BF16), ffn_w_out[0, 0].astype(BF16)
    xf = _ffn(xf, norm_g[0, 0], mod3(0, 0), w00_in, w00_out, t, tm)
    cf = _ffn(cf, norm_g[0, 0], mod3(0, 0, True), w00_in, w00_out, batch * tc, tm)

    w_in = mlstm_w_in[0]
    w_qt = (w_in[:, :qk] * (dk ** -0.5)).astype(BF16).T
    w_k = w_in[:, qk:2 * qk].astype(BF16)
    w_vt = w_in[:, 2 * qk:2 * qk + d].astype(BF16).T
    g0 = 2 * qk + d
    ng = 4 * HEADS
    w_gate = jnp.pad(w_in[:, g0:g0 + ng].astype(BF16), ((0, 0), (0, LANES - ng)))
    b_gate = jnp.pad(mlstm_b_gate[0], (0, LANES - ng)).reshape(1, LANES)
    w_o = w_in[:, g0 + ng:].astype(BF16)
    main, tr, gates = _proj(xf, norm_g[0, 1], mod3(0, 1), jnp.concatenate([w_k, w_o], axis=1),
                            jnp.concatenate([w_qt, w_vt], axis=0), w_gate, b_gate, t, min(PROJ_TM, t))
    main_c, tr_c, gates_c = _proj(cf, norm_g[0, 1], mod3(0, 1, True), w_k, w_vt, w_gate, b_gate,
                                  tc, min(PROJ_TM, tc))
    pcol, prow = _gateprep(gates, batch, SCAN_L)
    _, prowc = _gateprep(gates_c, batch, SCAN_L)
    y, w01_in, w01_out = _scan(main, tr, pcol, prow, main_c, tr_c, prowc, mlstm_norm_g[0], batch, t, tc, d,
                               cast=(ffn_w_in, ffn_w_out, 0, 1))
    xf, w10_in, w10_out = _outproj(y.reshape(batch * t, d), xf, mod3(0, 1), mlstm_w_out[0].astype(BF16), t, tmix,
                                   cast=(ffn_w_in, ffn_w_out, 1, 0))

    xf = _ffn(xf, norm_g[0, 2], mod3(0, 2), w01_in, w01_out, t, tm)

    xf = _ffn(xf, norm_g[1, 0], mod3(1, 0), w10_in, w10_out, t, tm)
    xf = _conv(xf, norm_g[1, 1], mod3(1, 1), conv_w_in[0].astype(BF16), conv_w[0],
               conv_w_out[0].astype(BF16), t, tm)
    xf = _ffn(xf, norm_g[1, 2], mod3(1, 2), ffn_w_in[1, 1].astype(BF16), ffn_w_out[1, 1].astype(BF16), t, tm,
              final_g=final_norm_g)
    return xf.reshape(batch, t, d)
```

```python
import functools

import jax
import jax.numpy as jnp
from jax import lax
from jax.experimental import pallas as pl
from jax.experimental.pallas import tpu as pltpu

F32 = jnp.float32
BF16 = jnp.bfloat16

HEADS = 8
N_MOD = 9
GRID_W = 64
RMS_EPS = 1e-6
M_INIT = -1e30

LANES = 128
MXU_COLS = 256
VMEM_LIMIT_BYTES = 60 * 1024 * 1024

NORM_ROWS = 64
EPI_ROWS = 256
ROW_TM = 1024
MIX_TM = 512
ROW_GROUPS = 4
OUT_TN = 512
FFN_TF = 512
CONV_TN = 512
PROJ_TN = 1024
PROJ_TM = 1024
SCAN_L = 256
RUNMAX_LANE = 4 * HEADS
SCAN_UNROLL_A = 4
SCAN_UNROLL_C = 4
MOD_TN = 1024
BF16_ROWS = 16
CAST_STEPS = 16
CAST_OUT_BLOCKS = 8


def _params(sem):
    return pltpu.CompilerParams(dimension_semantics=sem, vmem_limit_bytes=VMEM_LIMIT_BYTES)


def _sigmoid(x):
    return 1.0 / (1.0 + jnp.exp(-x))


def _norm_rows(x_ref, start, rows, gain, shift):
    rc = min(NORM_ROWS, rows)
    out = []
    for r0 in range(start, start + rows, rc):
        x = x_ref[r0:r0 + rc, :]
        inv = lax.rsqrt(jnp.mean(x * x, axis=-1, keepdims=True) + RMS_EPS)
        out.append((x_ref[r0:r0 + rc, :] * inv * gain + shift).astype(BF16))
    return jnp.concatenate(out, axis=0)


def _fill_xn(x_ref, xn_ref, g, mod):
    rows = x_ref.shape[0]
    rc = min(NORM_ROWS, rows)
    shift = mod[0:1]
    gain = g * (1.0 + mod[1:2])

    n = rows // rc

    def inv_rms(r):
        x = x_ref[pl.ds(pl.multiple_of(r * rc, rc), rc), :]
        return lax.rsqrt(jnp.mean(x * x, axis=-1, keepdims=True) + RMS_EPS)

    def body(r, inv):
        inv_next = inv_rms(jnp.minimum(r + 1, n - 1))
        sl = pl.ds(pl.multiple_of(r * rc, rc), rc)
        xn_ref[sl, :] = (x_ref[sl, :] * inv * gain + shift).astype(BF16)
        return inv_next

    lax.fori_loop(0, n, body, inv_rms(0))


def _mod_kernel(c_ref, w_ref, b_ref, o_ref):
    c = c_ref[...]
    s = (c * _sigmoid(c)).astype(BF16)
    w = w_ref[0].astype(BF16)
    o_ref[0] = jnp.dot(s, w, preferred_element_type=F32) + b_ref[0]


def _modulation(c_rows, w_mod, b_mod):
    depth, d, n = w_mod.shape
    r = c_rows.shape[0]
    return pl.pallas_call(
        _mod_kernel,
        grid=(depth, n // MOD_TN),
        in_specs=[
            pl.BlockSpec((r, d), lambda l, j: (0, 0)),
            pl.BlockSpec((1, d, MOD_TN), lambda l, j: (l, 0, j)),
            pl.BlockSpec((1, 1, MOD_TN), lambda l, j: (l, 0, j)),
        ],
        out_specs=pl.BlockSpec((1, r, MOD_TN), lambda l, j: (l, 0, j)),
        out_shape=jax.ShapeDtypeStruct((depth, r, n), F32),
        compiler_params=_params(("arbitrary", "arbitrary")),
        name="modulation",
    )(c_rows, w_mod, b_mod.reshape(depth, 1, n))


def _ffn_tile_start(j, f):
    assert f % LANES == 0 and f >= FFN_TF
    return pl.multiple_of(jnp.minimum(j * FFN_TF, f - FFN_TF), LANES)


def _ffn_kernel(*refs, f, final):
    if final:
        x_ref, g_ref, mod_ref, wg_ref, wu_ref, wo_ref, fg_ref, o_ref, xn_ref = refs
    else:
        x_ref, g_ref, mod_ref, wg_ref, wu_ref, wo_ref, o_ref, xn_ref = refs
    j = pl.program_id(1)
    nj = pl.num_programs(1)
    mod = mod_ref[0]
    shift, half_gate = mod[0:1], 0.5 * mod[2:3]
    gain = g_ref[...] * (1.0 + mod[1:2])
    d = o_ref.shape[1]
    rg = x_ref.shape[0] // ROW_GROUPS
    col = lax.broadcasted_iota(jnp.int32, (rg, FFN_TF), 1)

    def step(first, last):
        done = j * FFN_TF - _ffn_tile_start(j, f)
        acts = []
        for r in range(ROW_GROUPS):
            rows = slice(r * rg, (r + 1) * rg)
            if first:
                xn = _norm_rows(x_ref, r * rg, rg, gain, shift)
                xn_ref[rows, :] = xn
            else:
                xn = xn_ref[rows, :]
            hg = jnp.dot(xn, wg_ref[...], preferred_element_type=F32)
            hu = jnp.dot(xn, wu_ref[...], preferred_element_type=F32)
            acts.append((hg * _sigmoid(hg) * jnp.where(col >= done, hu, 0.0)).astype(BF16))
        for r in range(ROW_GROUPS):
            rows = slice(r * rg, (r + 1) * rg)
            for n0 in range(0, d, OUT_TN):
                cols = slice(n0, n0 + OUT_TN)
                upd = jnp.dot(acts[r], wo_ref[:, cols], preferred_element_type=F32)
                acc = upd if first else o_ref[rows, cols] + upd
                if last:
                    acc = x_ref[rows, cols] + half_gate[:, cols] * acc
                o_ref[rows, cols] = acc
            if last and final:
                y = o_ref[rows, :]
                ms = jnp.mean(y * y, axis=-1, keepdims=True)
                o_ref[rows, :] = y * lax.rsqrt(ms + RMS_EPS) * fg_ref[...]

    pl.when(j == 0)(lambda: step(True, False))
    pl.when((j > 0) & (j < nj - 1))(lambda: step(False, False))
    pl.when(j == nj - 1)(lambda: step(False, True))


def _ffn(x, g, mod3, w_in, w_out, rows_per_mod, tm, final_g=None):
    m, d = x.shape
    f = w_out.shape[0]
    nj = -(-f // FFN_TF)
    tpm = rows_per_mod // tm
    final = final_g is not None
    in_specs = [
        pl.BlockSpec((tm, d), lambda i, j: (i, 0)),
        pl.BlockSpec((1, d), lambda i, j: (0, 0)),
        pl.BlockSpec((1, 3, d), lambda i, j: (i // tpm, 0, 0)),
        pl.BlockSpec((pl.Element(d), pl.Element(FFN_TF)), lambda i, j: (0, _ffn_tile_start(j, f))),
        pl.BlockSpec((pl.Element(d), pl.Element(FFN_TF)),
                     lambda i, j: (0, pl.multiple_of(f + _ffn_tile_start(j, f), LANES))),
        pl.BlockSpec((pl.Element(FFN_TF), pl.Element(d)), lambda i, j: (_ffn_tile_start(j, f), 0)),
    ]
    args = [x, g.reshape(1, d), mod3, w_in, w_in, w_out]
    if final:
        in_specs.append(pl.BlockSpec((1, d), lambda i, j: (0, 0)))
        args.append(final_g.reshape(1, d))
    return pl.pallas_call(
        functools.partial(_ffn_kernel, f=f, final=final),
        grid=(m // tm, nj),
        in_specs=in_specs,
        out_specs=pl.BlockSpec((tm, d), lambda i, j: (i, 0)),
        out_shape=jax.ShapeDtypeStruct((m, d), F32),
        scratch_shapes=[pltpu.VMEM((tm, d), BF16)],
        compiler_params=_params(("parallel", "arbitrary")),
        name="ffn_final" if final else "ffn",
    )(*args)


def _cast_plan(w_in, w_out, layer, sub, nsteps, step):
    d, f2 = w_in.shape[2:]
    f = w_out.shape[2]
    rb = d // nsteps
    nblk = min(CAST_OUT_BLOCKS, nsteps)
    fb, per = f // nblk, nsteps // nblk
    assert rb * nsteps == d and rb % BF16_ROWS == 0
    assert fb * nblk == f and fb % BF16_ROWS == 0 and per * nblk == nsteps
    in_specs = [
        pl.BlockSpec((None, None, rb, f2), lambda *gi: (layer, sub, step(*gi), 0)),
        pl.BlockSpec((None, None, fb, d), lambda *gi: (layer, sub, step(*gi) // per, 0)),
    ]
    out_specs = [
        pl.BlockSpec((rb, f2), lambda *gi: (step(*gi), 0)),
        pl.BlockSpec((fb, d), lambda *gi: (step(*gi) // per, 0)),
    ]
    out_shape = [jax.ShapeDtypeStruct((d, f2), BF16), jax.ShapeDtypeStruct((f, d), BF16)]
    return in_specs, out_specs, out_shape


def _cast_blocks(src_in_ref, src_out_ref, dst_in_ref, dst_out_ref):
    dst_in_ref[...] = src_in_ref[...].astype(BF16)
    dst_out_ref[...] = src_out_ref[...].astype(BF16)


def _cast_ffn(w_in, w_out, layer, sub):
    cast_in, cast_out, cast_shape = _cast_plan(w_in, w_out, layer, sub, CAST_STEPS, lambda i: i)
    return pl.pallas_call(
        _cast_blocks,
        grid=(CAST_STEPS,),
        in_specs=cast_in,
        out_specs=cast_out,
        out_shape=cast_shape,
        compiler_params=_params(("arbitrary",)),
        name="cast_ffn_weights",
    )(w_in, w_out)


def _proj_kernel(x_ref, g_ref, mod_ref, w_ref, wt_ref, wg_ref, bg_ref,
                 main_ref, tr_ref, gates_ref, xn_ref, *, n_main, n_tr):
    j = pl.program_id(1)
    mod = mod_ref[0]
    rg = x_ref.shape[0] // ROW_GROUPS

    @pl.when(j == 0)
    def _():
        gain = g_ref[...] * (1.0 + mod[1:2])
        for r in range(ROW_GROUPS):
            rows = slice(r * rg, (r + 1) * rg)
            xn = _norm_rows(x_ref, r * rg, rg, gain, mod[0:1])
            xn_ref[rows, :] = xn
            gates_ref[rows, :] = jnp.dot(xn, wg_ref[...], preferred_element_type=F32) + bg_ref[...]
            main_ref[rows, :] = jnp.dot(xn, w_ref[...], preferred_element_type=F32).astype(BF16)
            tr = lax.dot_general(wt_ref[...], xn, (((1,), (1,)), ((), ())), preferred_element_type=F32)
            tr_ref[0, :, rows] = tr.astype(BF16)

    def main_tile():
        main_ref[...] = jnp.dot(xn_ref[...], w_ref[...], preferred_element_type=F32).astype(BF16)

    def tr_tile():
        tr = lax.dot_general(wt_ref[...], xn_ref[...], (((1,), (1,)), ((), ())), preferred_element_type=F32)
        tr_ref[0] = tr.astype(BF16)

    def both_tiles():
        main_tile()
        tr_tile()

    both = min(n_main, n_tr)
    if both > 1:
        pl.when((j > 0) & (j < both))(both_tiles)
    if n_main != n_tr:
        pl.when(j >= both)(main_tile if n_main > n_tr else tr_tile)


def _proj(x, g, mod3, w_main, w_tr, w_gate, b_gate, rows_per_batch, tm):
    m, d = x.shape
    n = w_main.shape[1]
    nt = w_tr.shape[0]
    n_main, n_tr = n // PROJ_TN, nt // PROJ_TN
    batch = m // rows_per_batch
    tpb = rows_per_batch // tm
    nmod = mod3.shape[0]
    mod_idx = (lambda i, j: (i // tpb, 0, 0)) if nmod > 1 else (lambda i, j: (0, 0, 0))
    return pl.pallas_call(
        functools.partial(_proj_kernel, n_main=n_main, n_tr=n_tr),
        grid=(m // tm, max(n_main, n_tr)),
        in_specs=[
            pl.BlockSpec((tm, d), lambda i, j: (i, 0)),
            pl.BlockSpec((1, d), lambda i, j: (0, 0)),
            pl.BlockSpec((1, 3, d), mod_idx),
            pl.BlockSpec((d, PROJ_TN), lambda i, j: (0, jnp.minimum(j, n_main - 1))),
            pl.BlockSpec((PROJ_TN, d), lambda i, j: (jnp.minimum(j, n_tr - 1), 0)),
            pl.BlockSpec((d, LANES), lambda i, j: (0, 0)),
            pl.BlockSpec((1, LANES), lambda i, j: (0, 0)),
        ],
        out_specs=[
            pl.BlockSpec((tm, PROJ_TN), lambda i, j: (i, jnp.minimum(j, n_main - 1))),
            pl.BlockSpec((1, PROJ_TN, tm), lambda i, j: (i // tpb, jnp.minimum(j, n_tr - 1), i % tpb)),
            pl.BlockSpec((tm, LANES), lambda i, j: (i, 0)),
        ],
        out_shape=[
            jax.ShapeDtypeStruct((m, n), BF16),
            jax.ShapeDtypeStruct((batch, nt, rows_per_batch), BF16),
            jax.ShapeDtypeStruct((m, LANES), F32),
        ],
        scratch_shapes=[pltpu.VMEM((tm, d), BF16)],
        compiler_params=_params(("parallel", "arbitrary")),
        name="mlstm_proj",
    )(x, g.reshape(1, d), mod3, w_main, w_tr, w_gate, b_gate)


def _tri_sum(tri, x):
    x1 = x.astype(BF16)
    r1 = x - x1.astype(F32)
    x2 = r1.astype(BF16)
    x3 = (r1 - x2.astype(F32)).astype(BF16)
    dot = functools.partial(jnp.dot, preferred_element_type=F32)
    return dot(tri, x1) + dot(tri, x2) + dot(tri, x3)


def _gateprep_kernel(g_ref, pcol_ref, prow_ref, *, chunk):
    t = g_ref.shape[1]
    row = lax.broadcasted_iota(jnp.int32, (chunk, chunk), 0)
    col = lax.broadcasted_iota(jnp.int32, (chunk, chunk), 1)
    tril = jnp.where(row >= col, 1.0, 0.0).astype(BF16)
    triu = jnp.where(row <= col, 1.0, 0.0).astype(BF16)
    lane = lax.broadcasted_iota(jnp.int32, (chunk, LANES), 1)
    f_fwd = (lane >= HEADS) & (lane < 2 * HEADS)
    f_bwd = (lane >= 3 * HEADS) & (lane < 4 * HEADS)
    tok = lax.broadcasted_iota(jnp.int32, (chunk, LANES), 0)
    run_lanes = ((lane >= RUNMAX_LANE) & (lane < RUNMAX_LANE + HEADS)) | (
        (lane >= RUNMAX_LANE + 2 * HEADS) & (lane < RUNMAX_LANE + 3 * HEADS))

    def body(ci, carry):
        sl = pl.ds(pl.multiple_of(ci * chunk, chunk), chunk)
        g = g_ref[0, sl, :]
        logf = jnp.minimum(g, 0.0) - jnp.log(1.0 + jnp.exp(-jnp.abs(g)))
        logf = jnp.where(f_fwd | f_bwd, logf, 0.0)
        cum = jnp.where(f_fwd, _tri_sum(tril, logf), jnp.where(f_bwd, _tri_sum(triu, logf), 0.0))
        p = jnp.where(f_fwd | f_bwd, cum, g - pltpu.roll(cum, LANES - HEADS, 1))
        pre, suf = p, p
        for sh in [1 << b for b in range(chunk.bit_length() - 1)]:
            pre = jnp.maximum(pre, jnp.where(tok >= sh, pltpu.roll(pre, sh, 0), -jnp.inf))
            suf = jnp.maximum(suf, jnp.where(tok < chunk - sh, pltpu.roll(suf, chunk - sh, 0), -jnp.inf))
        run = jnp.where(lane < 2 * HEADS, pre, suf)
        p = jnp.where(run_lanes, pltpu.roll(run, RUNMAX_LANE, 1), p)
        pcol_ref[0, sl, :] = p
        prow_ref[0, :, sl] = p.T
        return carry

    lax.fori_loop(0, t // chunk, body, 0)


def _gateprep(gates, batch, chunk):
    m = gates.shape[0]
    t = m // batch
    g3 = gates.reshape(batch, t, LANES)
    return pl.pallas_call(
        functools.partial(_gateprep_kernel, chunk=chunk),
        grid=(batch,),
        in_specs=[pl.BlockSpec((1, t, LANES), lambda b: (b, 0, 0))],
        out_specs=[
            pl.BlockSpec((1, t, LANES), lambda b: (b, 0, 0)),
            pl.BlockSpec((1, LANES, t), lambda b: (b, 0, 0)),
        ],
        out_shape=[
            jax.ShapeDtypeStruct((batch, t, LANES), F32),
            jax.ShapeDtypeStruct((batch, LANES, t), F32),
        ],
        compiler_params=_params(("parallel",)),
        name="gate_prep",
    )(g3)


def _scan_kernel(k_ref, o_ref, qt_ref, vt_ref, pcol_ref, prow_ref,
                 kc_ref, vtc_ref, prowc_ref, ng_ref, cast_in_ref, cast_out_ref,
                 y_ref, w_in_ref, w_out_ref, s_ref, min_ref, *, chunk, dk, dv):
    _cast_blocks(cast_in_ref, cast_out_ref, w_in_ref, w_out_ref)
    head = pl.program_id(1)
    t = k_ref.shape[1]
    tc = kc_ref.shape[1]
    nc = t // chunk
    ncc = tc // chunk
    lane = lax.broadcasted_iota(jnp.int32, (chunk, LANES), 1)
    row = lax.broadcasted_iota(jnp.int32, (chunk, chunk), 0)
    colm = lax.broadcasted_iota(jnp.int32, (chunk, chunk), 1)
    ones_t = jnp.ones((LANES, chunk), BF16)
    dot = functools.partial(jnp.dot, preferred_element_type=F32)

    def column(p, idx):
        return jnp.sum(jnp.where(lane == idx, p, 0.0), axis=1, keepdims=True)

    directions = ((0, 0, chunk - 1), (1, 2 * HEADS, 0))

    def gate_rows(pr_ref, t0, base):
        a_row = pr_ref[0, pl.ds(base + head, 1), pl.ds(t0, chunk)]
        c_row = pr_ref[0, pl.ds(base + HEADS + head, 1), pl.ds(t0, chunk)]
        return a_row, c_row

    def aug_t(vt):
        return jnp.concatenate([vt, ones_t], axis=0)

    def local_sum(k, vt, a_row, tot):
        w_end = tot + a_row
        mloc = jnp.max(w_end, axis=1, keepdims=True)
        w = jnp.exp(w_end - mloc)
        wv_t = jnp.concatenate([(vt.astype(F32) * w).astype(BF16),
                                jnp.broadcast_to(w, (LANES, chunk)).astype(BF16)], axis=0)
        return dot(wv_t, k), mloc

    def merge(caug, m, tot, g, mloc):
        m_new = jnp.maximum(tot + m, mloc)
        return jnp.exp(tot + m - m_new) * caug + jnp.exp(mloc - m_new) * g, m_new

    def scalar_tile(x):
        return jnp.broadcast_to(x, (8, LANES))

    states = []
    for d, base, last in directions:
        caug, m = jnp.zeros((dv + LANES, dk), F32), jnp.full((1, 1), M_INIT, F32)
        for ci in (range(ncc) if d == 0 else reversed(range(ncc))):
            t0 = ci * chunk
            a_row, c_row = gate_rows(prowc_ref, t0, base)
            tot = c_row[:, last:last + 1]
            g, mloc = local_sum(kc_ref[0, pl.ds(t0, chunk), :], vtc_ref[0, :, pl.ds(t0, chunk)], a_row, tot)
            caug, m = merge(caug, m, tot, g, mloc)
        states += [caug, m]

    def state_pass(step, carry):
        carry = list(carry)
        for d, base, last in directions:
            ci = step if d == 0 else nc - 1 - step
            t0 = pl.multiple_of(ci * chunk, chunk)
            caug, m = carry[2 * d], carry[2 * d + 1]
            s_ref[d, ci] = caug.astype(BF16)
            min_ref[d, ci] = scalar_tile(m)
            a_row, c_row = gate_rows(prow_ref, t0, base)
            tot = c_row[:, last:last + 1]
            g, mloc = local_sum(k_ref[0, pl.ds(t0, chunk), :], vt_ref[0, :, pl.ds(t0, chunk)], a_row, tot)
            carry[2 * d], carry[2 * d + 1] = merge(caug, m, tot, g, mloc)
        return tuple(carry)

    lax.fori_loop(0, nc, state_pass, tuple(states), unroll=SCAN_UNROLL_A)

    masks = (row <= colm, row >= colm)
    ng = ng_ref[...]

    def pass_c(ci, carry):
        t0 = pl.multiple_of(ci * chunk, chunk)
        sl = pl.ds(t0, chunk)
        qt = qt_ref[0, :, sl]
        vaug_t = aug_t(vt_ref[0, :, sl])
        s_t = dot(k_ref[0, sl, :], qt)
        pc = pcol_ref[0, sl, :]
        h_t = None
        for d, base, last in directions:
            a_col = column(pc, base + head)
            _, c_row = gate_rows(prow_ref, t0, base)
            run_row = prow_ref[0, pl.ds(RUNMAX_LANE + base + head, 1), sl]
            inter = c_row + min_ref[d, ci][0:1, 0:1]
            m_out = jnp.maximum(inter, c_row + run_row)
            p_t = (s_t * jnp.exp(jnp.where(masks[d], a_col + (c_row - m_out), -jnp.inf))).astype(BF16)
            qs = (qt.astype(F32) * jnp.exp(inter - m_out)).astype(BF16)
            r_t = dot(vaug_t, p_t) + dot(s_ref[d, ci], qs)
            inv = 1.0 / jnp.maximum(jnp.abs(r_t[dv:dv + 1]), jnp.exp(-m_out))
            hd = r_t[:dv] * inv
            h_t = hd if h_t is None else h_t + hd
        hn_t = h_t * lax.rsqrt(jnp.mean(h_t * h_t, axis=0, keepdims=True) + RMS_EPS)
        y_ref[0, sl, :] = (hn_t.T * ng * _sigmoid(o_ref[0, sl, :].astype(F32))).astype(BF16)
        return carry

    lax.fori_loop(0, nc, pass_c, 0, unroll=SCAN_UNROLL_C)


def _scan(main, tr, pcol, prow, main_c, tr_c, prowc, norm_g, batch, t, tc, d, cast):
    dv = d // HEADS
    dk = dv // 2
    wide0 = HEADS * dk // dv
    main3 = main.reshape(batch, t, main.shape[1])
    main_c3 = main_c.reshape(batch, tc, main_c.shape[1])
    nc = t // SCAN_L
    cast_in, cast_out, cast_shape = _cast_plan(*cast, batch * HEADS, lambda b, h: b * HEADS + h)
    return pl.pallas_call(
        functools.partial(_scan_kernel, chunk=SCAN_L, dk=dk, dv=dv),
        grid=(batch, HEADS),
        in_specs=[
            pl.BlockSpec((1, t, dk), lambda b, h: (b, 0, h)),
            pl.BlockSpec((1, t, dv), lambda b, h: (b, 0, wide0 + h)),
            pl.BlockSpec((1, dk, t), lambda b, h: (b, h, 0)),
            pl.BlockSpec((1, dv, t), lambda b, h: (b, wide0 + h, 0)),
            pl.BlockSpec((1, t, LANES), lambda b, h: (b, 0, 0)),
            pl.BlockSpec((1, LANES, t), lambda b, h: (b, 0, 0)),
            pl.BlockSpec((1, tc, dk), lambda b, h: (b, 0, h)),
            pl.BlockSpec((1, dv, tc), lambda b, h: (b, h, 0)),
            pl.BlockSpec((1, LANES, tc), lambda b, h: (b, 0, 0)),
            pl.BlockSpec((1, dv), lambda b, h: (0, h)),
        ] + cast_in,
        out_specs=[pl.BlockSpec((1, t, dv), lambda b, h: (b, 0, h))] + cast_out,
        out_shape=[jax.ShapeDtypeStruct((batch, t, d), BF16)] + cast_shape,
        scratch_shapes=[
            pltpu.VMEM((2, nc, dv + LANES, dk), BF16),
            pltpu.VMEM((2, nc, 8, LANES), F32),
        ],
        compiler_params=_params(("arbitrary", "arbitrary")),
        name="mlstm_scan",
    )(main3, main3, tr, tr, pcol, prow, main_c3, tr_c, prowc, norm_g.reshape(1, d), cast[0], cast[1])


def _outproj_kernel(y_ref, x_ref, mod_ref, w_ref, cast_in_ref, cast_out_ref, o_ref, w_in_ref, w_out_ref):
    _cast_blocks(cast_in_ref, cast_out_ref, w_in_ref, w_out_ref)
    gate = mod_ref[0][2:3]
    o_ref[...] = x_ref[...] + gate * jnp.dot(y_ref[...], w_ref[...], preferred_element_type=F32)


def _outproj(y, x, mod3, w, rows_per_mod, tm, cast):
    m, d = x.shape
    tpm = rows_per_mod // tm
    cast_in, cast_out, cast_shape = _cast_plan(*cast, m // tm, lambda i: i)
    return pl.pallas_call(
        _outproj_kernel,
        grid=(m // tm,),
        in_specs=[
            pl.BlockSpec((tm, d), lambda i: (i, 0)),
            pl.BlockSpec((tm, d), lambda i: (i, 0)),
            pl.BlockSpec((1, 3, d), lambda i: (i // tpm, 0, 0)),
            pl.BlockSpec((d, d), lambda i: (0, 0)),
        ] + cast_in,
        out_specs=[pl.BlockSpec((tm, d), lambda i: (i, 0))] + cast_out,
        out_shape=[jax.ShapeDtypeStruct((m, d), F32)] + cast_shape,
        compiler_params=_params(("arbitrary",)),
        name="mlstm_out",
    )(y, x, mod3, w, cast[0], cast[1])


def _conv_kernel(x_ref, g_ref, mod_ref, wb_ref, wc_ref, wu_ref, cw_ref, wo_ref, o_ref, xn_ref, *, tn):
    j = pl.program_id(1)
    nj = pl.num_programs(1)
    mod = mod_ref[0]
    shift, gate = mod[0:1], mod[2:3]
    gain = g_ref[...] * (1.0 + mod[1:2])
    d = o_ref.shape[1]
    rg = x_ref.shape[0] // ROW_GROUPS
    cw = cw_ref[...]
    pos = lax.broadcasted_iota(jnp.int32, (rg, tn), 0) % GRID_W

    def step(first, last):
        acts = []
        for r in range(ROW_GROUPS):
            rows = slice(r * rg, (r + 1) * rg)
            if first:
                xn = _norm_rows(x_ref, r * rg, rg, gain, shift)
                xn_ref[rows, :] = xn
            else:
                xn = xn_ref[rows, :]
            bg = jnp.dot(xn, wb_ref[...], preferred_element_type=F32)
            cg = jnp.dot(xn, wc_ref[...], preferred_element_type=F32)
            u = jnp.dot(xn, wu_ref[...], preferred_element_type=F32)
            z = cg * u
            z_prev = jnp.where(pos == 0, 0.0, pltpu.roll(z, 1, 0))
            z_next = jnp.where(pos == GRID_W - 1, 0.0, pltpu.roll(z, rg - 1, 0))
            zc = cw[0:1] * z_prev + cw[1:2] * z + cw[2:3] * z_next
            acts.append((bg * zc).astype(BF16))
        for r in range(ROW_GROUPS):
            rows = slice(r * rg, (r + 1) * rg)
            for n0 in range(0, d, OUT_TN):
                cols = slice(n0, n0 + OUT_TN)
                upd = jnp.dot(acts[r], wo_ref[:, cols], preferred_element_type=F32)
                acc = upd if first else o_ref[rows, cols] + upd
                if last:
                    acc = x_ref[rows, cols] + gate[:, cols] * acc
                o_ref[rows, cols] = acc

    pl.when(j == 0)(lambda: step(True, False))
    pl.when((j > 0) & (j < nj - 1))(lambda: step(False, False))
    pl.when(j == nj - 1)(lambda: step(False, True))


def _conv(x, g, mod3, w3, cw, wo, rows_per_mod, tm):
    m, d = x.shape
    nj = d // CONV_TN
    tpm = rows_per_mod // tm
    return pl.pallas_call(
        functools.partial(_conv_kernel, tn=CONV_TN),
        grid=(m // tm, nj),
        in_specs=[
            pl.BlockSpec((tm, d), lambda i, j: (i, 0)),
            pl.BlockSpec((1, d), lambda i, j: (0, 0)),
            pl.BlockSpec((1, 3, d), lambda i, j: (i // tpm, 0, 0)),
            pl.BlockSpec((d, CONV_TN), lambda i, j: (0, j)),
            pl.BlockSpec((d, CONV_TN), lambda i, j: (0, nj + j)),
            pl.BlockSpec((d, CONV_TN), lambda i, j: (0, 2 * nj + j)),
            pl.BlockSpec((3, CONV_TN), lambda i, j: (0, j)),
            pl.BlockSpec((CONV_TN, d), lambda i, j: (j, 0)),
        ],
        out_specs=pl.BlockSpec((tm, d), lambda i, j: (i, 0)),
        out_shape=jax.ShapeDtypeStruct((m, d), F32),
        scratch_shapes=[pltpu.VMEM((tm, d), BF16)],
        compiler_params=_params(("parallel", "arbitrary")),
        name="conv_mixer",
    )(x, g.reshape(1, d), mod3, w3, w3, w3, cw, wo)


def kernel(x, c, ctx, c_ctx, w_mod, b_mod, norm_g, ffn_w_in, ffn_w_out, mlstm_w_in, mlstm_b_gate,
           mlstm_norm_g, mlstm_w_out, conv_w_in, conv_w, conv_w_out, final_norm_g):
    batch, t, d = x.shape
    tc = ctx.shape[1]
    depth = w_mod.shape[0]
    assert depth == 2 and d % (2 * HEADS * LANES) == 0 and t % GRID_W == 0
    dv = d // HEADS
    dk = dv // 2
    qk = HEADS * dk

    c_rows = jnp.concatenate([c, c_ctx[None, :], jnp.zeros((-(batch + 1) % 8, d), F32)], axis=0)
    mod = _modulation(c_rows, w_mod, b_mod).reshape(depth, c_rows.shape[0], N_MOD, d)

    def mod3(layer, sub, context=False):
        rows = mod[layer, batch:batch + 1] if context else mod[layer, :batch]
        return rows[:, 3 * sub:3 * sub + 3, :]

    tm = min(ROW_TM, t)
    tmix = min(MIX_TM, t)
    xf = x.reshape(batch * t, d)
    cf = ctx.reshape(batch * tc, d)

    w00_in, w00_out = _cast_ffn(ffn_w_in, ffn_w_out, 0, 0)
    xf = _ffn(xf, norm_g[0, 0], mod3(0, 0), w00_in, w00_out, t, tm)
    cf = _ffn(cf, norm_g[0, 0], mod3(0, 0, True), w00_in, w00_out, batch * tc, tm)

    w_in = mlstm_w_in[0]
    w_qt = (w_in[:, :qk] * (dk ** -0.5)).astype(BF16).T
    w_k = w_in[:, qk:2 * qk].astype(BF16)
    w_vt = w_in[:, 2 * qk:2 * qk + d].astype(BF16).T
    g0 = 2 * qk + d
    ng = 4 * HEADS
    w_gate = jnp.pad(w_in[:, g0:g0 + ng].astype(BF16), ((0, 0), (0, LANES - ng)))
    b_gate = jnp.pad(mlstm_b_gate[0], (0, LANES - ng)).reshape(1, LANES)
    w_o = w_in[:, g0 + ng:].astype(BF16)
    main, tr, gates = _proj(xf, norm_g[0, 1], mod3(0, 1), jnp.concatenate([w_k, w_o], axis=1),
                            jnp.concatenate([w_qt, w_vt], axis=0), w_gate, b_gate, t, min(PROJ_TM, t))
    main_c, tr_c, gates_c = _proj(cf, norm_g[0, 1], mod3(0, 1, True), w_k, w_vt, w_gate, b_gate,
                                  tc, min(PROJ_TM, tc))
    pcol, prow = _gateprep(gates, batch, SCAN_L)
    _, prowc = _gateprep(gates_c, batch, SCAN_L)
    y, w01_in, w01_out = _scan(main, tr, pcol, prow, main_c, tr_c, prowc, mlstm_norm_g[0], batch, t, tc, d,
                               cast=(ffn_w_in, ffn_w_out, 0, 1))
    xf, w10_in, w10_out = _outproj(y.reshape(batch * t, d), xf, mod3(0, 1), mlstm_w_out[0].astype(BF16), t, tmix,
                                   cast=(ffn_w_in, ffn_w_out, 1, 0))

    xf = _ffn(xf, norm_g[0, 2], mod3(0, 2), w01_in, w01_out, t, tm)

    xf = _ffn(xf, norm_g[1, 0], mod3(1, 0), w10_in, w10_out, t, tm)
    xf = _conv(xf, norm_g[1, 1], mod3(1, 1), conv_w_in[0].astype(BF16), conv_w[0],
               conv_w_out[0].astype(BF16), t, tm)
    w11_in, w11_out = _cast_ffn(ffn_w_in, ffn_w_out, 1, 1)
    xf = _ffn(xf, norm_g[1, 2], mod3(1, 2), w11_in, w11_out, t, tm, final_g=final_norm_g)
    return xf.reshape(batch, t, d)
```

```python
import functools

import jax
import jax.numpy as jnp
from jax import lax
from jax.experimental import pallas as pl
from jax.experimental.pallas import tpu as pltpu

F32 = jnp.float32
BF16 = jnp.bfloat16

HEADS = 8
N_MOD = 9
GRID_W = 64
RMS_EPS = 1e-6
M_INIT = -1e30

LANES = 128
MXU_COLS = 256
VMEM_LIMIT_BYTES = 60 * 1024 * 1024

NORM_ROWS = 64
EPI_ROWS = 256
ROW_TM = 1024
MIX_TM = 512
ROW_GROUPS = 4
OUT_TN = 512
FFN_TF = 512
CONV_TN = 512
PROJ_TN = 1024
PROJ_TM = 1024
SCAN_L = 256
RUNMAX_LANE = 4 * HEADS
SCAN_UNROLL_A = 4
SCAN_UNROLL_C = 4
MOD_TN = 1024
BF16_ROWS = 16
CAST_STEPS = 16
CAST_OUT_BLOCKS = 8


def _params(sem):
    return pltpu.CompilerParams(dimension_semantics=sem, vmem_limit_bytes=VMEM_LIMIT_BYTES)


def _sigmoid(x):
    return 1.0 / (1.0 + jnp.exp(-x))


def _norm_rows(x_ref, start, rows, gain, shift):
    rc = min(NORM_ROWS, rows)
    out = []
    for r0 in range(start, start + rows, rc):
        x = x_ref[r0:r0 + rc, :]
        inv = lax.rsqrt(jnp.mean(x * x, axis=-1, keepdims=True) + RMS_EPS)
        out.append((x_ref[r0:r0 + rc, :] * inv * gain + shift).astype(BF16))
    return jnp.concatenate(out, axis=0)


def _fill_xn(x_ref, xn_ref, g, mod):
    rows = x_ref.shape[0]
    rc = min(NORM_ROWS, rows)
    shift = mod[0:1]
    gain = g * (1.0 + mod[1:2])

    n = rows // rc

    def inv_rms(r):
        x = x_ref[pl.ds(pl.multiple_of(r * rc, rc), rc), :]
        return lax.rsqrt(jnp.mean(x * x, axis=-1, keepdims=True) + RMS_EPS)

    def body(r, inv):
        inv_next = inv_rms(jnp.minimum(r + 1, n - 1))
        sl = pl.ds(pl.multiple_of(r * rc, rc), rc)
        xn_ref[sl, :] = (x_ref[sl, :] * inv * gain + shift).astype(BF16)
        return inv_next

    lax.fori_loop(0, n, body, inv_rms(0))


def _mod_kernel(c_ref, w_ref, b_ref, o_ref):
    c = c_ref[...]
    s = (c * _sigmoid(c)).astype(BF16)
    w = w_ref[0].astype(BF16)
    o_ref[0] = jnp.dot(s, w, preferred_element_type=F32) + b_ref[0]


def _modulation(c_rows, w_mod, b_mod):
    depth, d, n = w_mod.shape
    r = c_rows.shape[0]
    return pl.pallas_call(
        _mod_kernel,
        grid=(depth, n // MOD_TN),
        in_specs=[
            pl.BlockSpec((r, d), lambda l, j: (0, 0)),
            pl.BlockSpec((1, d, MOD_TN), lambda l, j: (l, 0, j)),
            pl.BlockSpec((1, 1, MOD_TN), lambda l, j: (l, 0, j)),
        ],
        out_specs=pl.BlockSpec((1, r, MOD_TN), lambda l, j: (l, 0, j)),
        out_shape=jax.ShapeDtypeStruct((depth, r, n), F32),
        compiler_params=_params(("arbitrary", "arbitrary")),
        name="modulation",
    )(c_rows, w_mod, b_mod.reshape(depth, 1, n))


def _ffn_tile_start(j, f):
    assert f % LANES == 0 and f >= FFN_TF
    return pl.multiple_of(jnp.minimum(j * FFN_TF, f - FFN_TF), LANES)


def _ffn_kernel(*refs, f, final):
    if final:
        x_ref, g_ref, mod_ref, wg_ref, wu_ref, wo_ref, fg_ref, o_ref, xn_ref = refs
    else:
        x_ref, g_ref, mod_ref, wg_ref, wu_ref, wo_ref, o_ref, xn_ref = refs
    j = pl.program_id(1)
    nj = pl.num_programs(1)
    mod = mod_ref[0]
    shift, half_gate = mod[0:1], 0.5 * mod[2:3]
    gain = g_ref[...] * (1.0 + mod[1:2])
    d = o_ref.shape[1]
    rg = x_ref.shape[0] // ROW_GROUPS
    col = lax.broadcasted_iota(jnp.int32, (rg, FFN_TF), 1)

    def step(first, last):
        done = j * FFN_TF - _ffn_tile_start(j, f)
        acts = []
        for r in range(ROW_GROUPS):
            rows = slice(r * rg, (r + 1) * rg)
            if first:
                xn = _norm_rows(x_ref, r * rg, rg, gain, shift)
                xn_ref[rows, :] = xn
            else:
                xn = xn_ref[rows, :]
            hg = jnp.dot(xn, wg_ref[...], preferred_element_type=F32)
            hu = jnp.dot(xn, wu_ref[...], preferred_element_type=F32)
            if last:
                hu = jnp.where(col >= done, hu, 0.0)
            acts.append((hg * _sigmoid(hg) * hu).astype(BF16))
        for r in range(ROW_GROUPS):
            rows = slice(r * rg, (r + 1) * rg)
            for n0 in range(0, d, OUT_TN):
                cols = slice(n0, n0 + OUT_TN)
                upd = jnp.dot(acts[r], wo_ref[:, cols], preferred_element_type=F32)
                acc = upd if first else o_ref[rows, cols] + upd
                if last:
                    acc = x_ref[rows, cols] + half_gate[:, cols] * acc
                o_ref[rows, cols] = acc
            if last and final:
                y = o_ref[rows, :]
                ms = jnp.mean(y * y, axis=-1, keepdims=True)
                o_ref[rows, :] = y * lax.rsqrt(ms + RMS_EPS) * fg_ref[...]

    pl.when(j == 0)(lambda: step(True, False))
    pl.when((j > 0) & (j < nj - 1))(lambda: step(False, False))
    pl.when(j == nj - 1)(lambda: step(False, True))


def _ffn(x, g, mod3, w_in, w_out, rows_per_mod, tm, final_g=None):
    m, d = x.shape
    f = w_out.shape[0]
    nj = -(-f // FFN_TF)
    tpm = rows_per_mod // tm
    final = final_g is not None
    in_specs = [
        pl.BlockSpec((tm, d), lambda i, j: (i, 0)),
        pl.BlockSpec((1, d), lambda i, j: (0, 0)),
        pl.BlockSpec((1, 3, d), lambda i, j: (i // tpm, 0, 0)),
        pl.BlockSpec((pl.Element(d), pl.Element(FFN_TF)), lambda i, j: (0, _ffn_tile_start(j, f))),
        pl.BlockSpec((pl.Element(d), pl.Element(FFN_TF)),
                     lambda i, j: (0, pl.multiple_of(f + _ffn_tile_start(j, f), LANES))),
        pl.BlockSpec((pl.Element(FFN_TF), pl.Element(d)), lambda i, j: (_ffn_tile_start(j, f), 0)),
    ]
    args = [x, g.reshape(1, d), mod3, w_in, w_in, w_out]
    if final:
        in_specs.append(pl.BlockSpec((1, d), lambda i, j: (0, 0)))
        args.append(final_g.reshape(1, d))
    return pl.pallas_call(
        functools.partial(_ffn_kernel, f=f, final=final),
        grid=(m // tm, nj),
        in_specs=in_specs,
        out_specs=pl.BlockSpec((tm, d), lambda i, j: (i, 0)),
        out_shape=jax.ShapeDtypeStruct((m, d), F32),
        scratch_shapes=[pltpu.VMEM((tm, d), BF16)],
        compiler_params=_params(("parallel", "arbitrary")),
        name="ffn_final" if final else "ffn",
    )(*args)


def _cast_plan(w_in, w_out, layer, sub, nsteps, step):
    d, f2 = w_in.shape[2:]
    f = w_out.shape[2]
    rb = d // nsteps
    nblk = min(CAST_OUT_BLOCKS, nsteps)
    fb, per = f // nblk, nsteps // nblk
    assert rb * nsteps == d and rb % BF16_ROWS == 0
    assert fb * nblk == f and fb % BF16_ROWS == 0 and per * nblk == nsteps
    in_specs = [
        pl.BlockSpec((None, None, rb, f2), lambda *gi: (layer, sub, step(*gi), 0)),
        pl.BlockSpec((None, None, fb, d), lambda *gi: (layer, sub, step(*gi) // per, 0)),
    ]
    out_specs = [
        pl.BlockSpec((rb, f2), lambda *gi: (step(*gi), 0)),
        pl.BlockSpec((fb, d), lambda *gi: (step(*gi) // per, 0)),
    ]
    out_shape = [jax.ShapeDtypeStruct((d, f2), BF16), jax.ShapeDtypeStruct((f, d), BF16)]
    return in_specs, out_specs, out_shape


def _cast_blocks(src_in_ref, src_out_ref, dst_in_ref, dst_out_ref):
    dst_in_ref[...] = src_in_ref[...].astype(BF16)
    dst_out_ref[...] = src_out_ref[...].astype(BF16)


def _cast_ffn(w_in, w_out, layer, sub):
    cast_in, cast_out, cast_shape = _cast_plan(w_in, w_out, layer, sub, CAST_STEPS, lambda i: i)
    return pl.pallas_call(
        _cast_blocks,
        grid=(CAST_STEPS,),
        in_specs=cast_in,
        out_specs=cast_out,
        out_shape=cast_shape,
        compiler_params=_params(("arbitrary",)),
        name="cast_ffn_weights",
    )(w_in, w_out)


def _proj_kernel(x_ref, g_ref, mod_ref, w_ref, wt_ref, wg_ref, bg_ref,
                 main_ref, tr_ref, gates_ref, xn_ref, *, n_main, n_tr):
    j = pl.program_id(1)
    mod = mod_ref[0]
    rg = x_ref.shape[0] // ROW_GROUPS

    @pl.when(j == 0)
    def _():
        gain = g_ref[...] * (1.0 + mod[1:2])
        for r in range(ROW_GROUPS):
            rows = slice(r * rg, (r + 1) * rg)
            xn = _norm_rows(x_ref, r * rg, rg, gain, mod[0:1])
            xn_ref[rows, :] = xn
            gates_ref[rows, :] = jnp.dot(xn, wg_ref[...], preferred_element_type=F32) + bg_ref[...]
            main_ref[rows, :] = jnp.dot(xn, w_ref[...], preferred_element_type=F32).astype(BF16)
            tr = lax.dot_general(wt_ref[...], xn, (((1,), (1,)), ((), ())), preferred_element_type=F32)
            tr_ref[0, :, rows] = tr.astype(BF16)

    def main_tile():
        main_ref[...] = jnp.dot(xn_ref[...], w_ref[...], preferred_element_type=F32).astype(BF16)

    def tr_tile():
        tr = lax.dot_general(wt_ref[...], xn_ref[...], (((1,), (1,)), ((), ())), preferred_element_type=F32)
        tr_ref[0] = tr.astype(BF16)

    def both_tiles():
        main_tile()
        tr_tile()

    both = min(n_main, n_tr)
    if both > 1:
        pl.when((j > 0) & (j < both))(both_tiles)
    if n_main != n_tr:
        pl.when(j >= both)(main_tile if n_main > n_tr else tr_tile)


def _proj(x, g, mod3, w_main, w_tr, w_gate, b_gate, rows_per_batch, tm):
    m, d = x.shape
    n = w_main.shape[1]
    nt = w_tr.shape[0]
    n_main, n_tr = n // PROJ_TN, nt // PROJ_TN
    batch = m // rows_per_batch
    tpb = rows_per_batch // tm
    nmod = mod3.shape[0]
    mod_idx = (lambda i, j: (i // tpb, 0, 0)) if nmod > 1 else (lambda i, j: (0, 0, 0))
    return pl.pallas_call(
        functools.partial(_proj_kernel, n_main=n_main, n_tr=n_tr),
        grid=(m // tm, max(n_main, n_tr)),
        in_specs=[
            pl.BlockSpec((tm, d), lambda i, j: (i, 0)),
            pl.BlockSpec((1, d), lambda i, j: (0, 0)),
            pl.BlockSpec((1, 3, d), mod_idx),
            pl.BlockSpec((d, PROJ_TN), lambda i, j: (0, jnp.minimum(j, n_main - 1))),
            pl.BlockSpec((PROJ_TN, d), lambda i, j: (jnp.minimum(j, n_tr - 1), 0)),
            pl.BlockSpec((d, LANES), lambda i, j: (0, 0)),
            pl.BlockSpec((1, LANES), lambda i, j: (0, 0)),
        ],
        out_specs=[
            pl.BlockSpec((tm, PROJ_TN), lambda i, j: (i, jnp.minimum(j, n_main - 1))),
            pl.BlockSpec((1, PROJ_TN, tm), lambda i, j: (i // tpb, jnp.minimum(j, n_tr - 1), i % tpb)),
            pl.BlockSpec((tm, LANES), lambda i, j: (i, 0)),
        ],
        out_shape=[
            jax.ShapeDtypeStruct((m, n), BF16),
            jax.ShapeDtypeStruct((batch, nt, rows_per_batch), BF16),
            jax.ShapeDtypeStruct((m, LANES), F32),
        ],
        scratch_shapes=[pltpu.VMEM((tm, d), BF16)],
        compiler_params=_params(("parallel", "arbitrary")),
        name="mlstm_proj",
    )(x, g.reshape(1, d), mod3, w_main, w_tr, w_gate, b_gate)


def _tri_sum(tri, x):
    x1 = x.astype(BF16)
    r1 = x - x1.astype(F32)
    x2 = r1.astype(BF16)
    x3 = (r1 - x2.astype(F32)).astype(BF16)
    dot = functools.partial(jnp.dot, preferred_element_type=F32)
    return dot(tri, x1) + dot(tri, x2) + dot(tri, x3)


def _gateprep_kernel(g_ref, pcol_ref, prow_ref, *, chunk):
    t = g_ref.shape[1]
    row = lax.broadcasted_iota(jnp.int32, (chunk, chunk), 0)
    col = lax.broadcasted_iota(jnp.int32, (chunk, chunk), 1)
    tril = jnp.where(row >= col, 1.0, 0.0).astype(BF16)
    triu = jnp.where(row <= col, 1.0, 0.0).astype(BF16)
    lane = lax.broadcasted_iota(jnp.int32, (chunk, LANES), 1)
    f_fwd = (lane >= HEADS) & (lane < 2 * HEADS)
    f_bwd = (lane >= 3 * HEADS) & (lane < 4 * HEADS)
    tok = lax.broadcasted_iota(jnp.int32, (chunk, LANES), 0)
    run_lanes = ((lane >= RUNMAX_LANE) & (lane < RUNMAX_LANE + HEADS)) | (
        (lane >= RUNMAX_LANE + 2 * HEADS) & (lane < RUNMAX_LANE + 3 * HEADS))

    def body(ci, carry):
        sl = pl.ds(pl.multiple_of(ci * chunk, chunk), chunk)
        g = g_ref[0, sl, :]
        logf = jnp.minimum(g, 0.0) - jnp.log(1.0 + jnp.exp(-jnp.abs(g)))
        logf = jnp.where(f_fwd | f_bwd, logf, 0.0)
        cum = jnp.where(f_fwd, _tri_sum(tril, logf), jnp.where(f_bwd, _tri_sum(triu, logf), 0.0))
        p = jnp.where(f_fwd | f_bwd, cum, g - pltpu.roll(cum, LANES - HEADS, 1))
        pre, suf = p, p
        for sh in [1 << b for b in range(chunk.bit_length() - 1)]:
            pre = jnp.maximum(pre, jnp.where(tok >= sh, pltpu.roll(pre, sh, 0), -jnp.inf))
            suf = jnp.maximum(suf, jnp.where(tok < chunk - sh, pltpu.roll(suf, chunk - sh, 0), -jnp.inf))
        run = jnp.where(lane < 2 * HEADS, pre, suf)
        p = jnp.where(run_lanes, pltpu.roll(run, RUNMAX_LANE, 1), p)
        pcol_ref[0, sl, :] = p
        prow_ref[0, :, sl] = p.T
        return carry

    lax.fori_loop(0, t // chunk, body, 0)


def _gateprep(gates, batch, chunk):
    m = gates.shape[0]
    t = m // batch
    g3 = gates.reshape(batch, t, LANES)
    return pl.pallas_call(
        functools.partial(_gateprep_kernel, chunk=chunk),
        grid=(batch,),
        in_specs=[pl.BlockSpec((1, t, LANES), lambda b: (b, 0, 0))],
        out_specs=[
            pl.BlockSpec((1, t, LANES), lambda b: (b, 0, 0)),
            pl.BlockSpec((1, LANES, t), lambda b: (b, 0, 0)),
        ],
        out_shape=[
            jax.ShapeDtypeStruct((batch, t, LANES), F32),
            jax.ShapeDtypeStruct((batch, LANES, t), F32),
        ],
        compiler_params=_params(("parallel",)),
        name="gate_prep",
    )(g3)


def _scan_kernel(k_ref, o_ref, qt_ref, vt_ref, pcol_ref, prow_ref,
                 kc_ref, vtc_ref, prowc_ref, ng_ref, cast_in_ref, cast_out_ref,
                 y_ref, w_in_ref, w_out_ref, s_ref, min_ref, *, chunk, dk, dv):
    _cast_blocks(cast_in_ref, cast_out_ref, w_in_ref, w_out_ref)
    head = pl.program_id(1)
    t = k_ref.shape[1]
    tc = kc_ref.shape[1]
    nc = t // chunk
    ncc = tc // chunk
    lane = lax.broadcasted_iota(jnp.int32, (chunk, LANES), 1)
    row = lax.broadcasted_iota(jnp.int32, (chunk, chunk), 0)
    colm = lax.broadcasted_iota(jnp.int32, (chunk, chunk), 1)
    ones_t = jnp.ones((LANES, chunk), BF16)
    dot = functools.partial(jnp.dot, preferred_element_type=F32)

    def column(p, idx):
        return jnp.sum(jnp.where(lane == idx, p, 0.0), axis=1, keepdims=True)

    directions = ((0, 0, chunk - 1), (1, 2 * HEADS, 0))

    def gate_rows(pr_ref, t0, base):
        a_row = pr_ref[0, pl.ds(base + head, 1), pl.ds(t0, chunk)]
        c_row = pr_ref[0, pl.ds(base + HEADS + head, 1), pl.ds(t0, chunk)]
        return a_row, c_row

    def aug_t(vt):
        return jnp.concatenate([vt, ones_t], axis=0)

    def local_sum(k, vt, a_row, tot):
        w_end = tot + a_row
        mloc = jnp.max(w_end, axis=1, keepdims=True)
        w = jnp.exp(w_end - mloc)
        wv_t = jnp.concatenate([(vt.astype(F32) * w).astype(BF16),
                                jnp.broadcast_to(w, (LANES, chunk)).astype(BF16)], axis=0)
        return dot(wv_t, k), mloc

    def merge(caug, m, tot, g, mloc):
        m_new = jnp.maximum(tot + m, mloc)
        return jnp.exp(tot + m - m_new) * caug + jnp.exp(mloc - m_new) * g, m_new

    def scalar_tile(x):
        return jnp.broadcast_to(x, (8, LANES))

    states = []
    for d, base, last in directions:
        caug, m = jnp.zeros((dv + LANES, dk), F32), jnp.full((1, 1), M_INIT, F32)
        for ci in (range(ncc) if d == 0 else reversed(range(ncc))):
            t0 = ci * chunk
            a_row, c_row = gate_rows(prowc_ref, t0, base)
            tot = c_row[:, last:last + 1]
            g, mloc = local_sum(kc_ref[0, pl.ds(t0, chunk), :], vtc_ref[0, :, pl.ds(t0, chunk)], a_row, tot)
            caug, m = merge(caug, m, tot, g, mloc)
        states += [caug, m]

    def state_pass(step, carry):
        carry = list(carry)
        for d, base, last in directions:
            ci = step if d == 0 else nc - 1 - step
            t0 = pl.multiple_of(ci * chunk, chunk)
            caug, m = carry[2 * d], carry[2 * d + 1]
            s_ref[d, ci] = caug.astype(BF16)
            min_ref[d, ci] = scalar_tile(m)
            a_row, c_row = gate_rows(prow_ref, t0, base)
            tot = c_row[:, last:last + 1]
            g, mloc = local_sum(k_ref[0, pl.ds(t0, chunk), :], vt_ref[0, :, pl.ds(t0, chunk)], a_row, tot)
            carry[2 * d], carry[2 * d + 1] = merge(caug, m, tot, g, mloc)
        return tuple(carry)

    lax.fori_loop(0, nc, state_pass, tuple(states), unroll=SCAN_UNROLL_A)

    masks = (row <= colm, row >= colm)
    ng = ng_ref[...]

    def pass_c(ci, carry):
        t0 = pl.multiple_of(ci * chunk, chunk)
        sl = pl.ds(t0, chunk)
        qt = qt_ref[0, :, sl]
        vaug_t = aug_t(vt_ref[0, :, sl])
        s_t = dot(k_ref[0, sl, :], qt)
        pc = pcol_ref[0, sl, :]
        h_t = None
        for d, base, last in directions:
            a_col = column(pc, base + head)
            _, c_row = gate_rows(prow_ref, t0, base)
            run_row = prow_ref[0, pl.ds(RUNMAX_LANE + base + head, 1), sl]
            inter = c_row + min_ref[d, ci][0:1, 0:1]
            m_out = jnp.maximum(inter, c_row + run_row)
            p_t = (s_t * jnp.exp(jnp.where(masks[d], a_col + (c_row - m_out), -jnp.inf))).astype(BF16)
            qs = (qt.astype(F32) * jnp.exp(inter - m_out)).astype(BF16)
            r_t = dot(vaug_t, p_t) + dot(s_ref[d, ci], qs)
            inv = 1.0 / jnp.maximum(jnp.abs(r_t[dv:dv + 1]), jnp.exp(-m_out))
            hd = r_t[:dv] * inv
            h_t = hd if h_t is None else h_t + hd
        hn_t = h_t * lax.rsqrt(jnp.mean(h_t * h_t, axis=0, keepdims=True) + RMS_EPS)
        y_ref[0, sl, :] = (hn_t.T * ng * _sigmoid(o_ref[0, sl, :].astype(F32))).astype(BF16)
        return carry

    lax.fori_loop(0, nc, pass_c, 0, unroll=SCAN_UNROLL_C)


def _scan(main, tr, pcol, prow, main_c, tr_c, prowc, norm_g, batch, t, tc, d, cast):
    dv = d // HEADS
    dk = dv // 2
    wide0 = HEADS * dk // dv
    main3 = main.reshape(batch, t, main.shape[1])
    main_c3 = main_c.reshape(batch, tc, main_c.shape[1])
    nc = t // SCAN_L
    cast_in, cast_out, cast_shape = _cast_plan(*cast, batch * HEADS, lambda b, h: b * HEADS + h)
    return pl.pallas_call(
        functools.partial(_scan_kernel, chunk=SCAN_L, dk=dk, dv=dv),
        grid=(batch, HEADS),
        in_specs=[
            pl.BlockSpec((1, t, dk), lambda b, h: (b, 0, h)),
            pl.BlockSpec((1, t, dv), lambda b, h: (b, 0, wide0 + h)),
            pl.BlockSpec((1, dk, t), lambda b, h: (b, h, 0)),
            pl.BlockSpec((1, dv, t), lambda b, h: (b, wide0 + h, 0)),
            pl.BlockSpec((1, t, LANES), lambda b, h: (b, 0, 0)),
            pl.BlockSpec((1, LANES, t), lambda b, h: (b, 0, 0)),
            pl.BlockSpec((1, tc, dk), lambda b, h: (b, 0, h)),
            pl.BlockSpec((1, dv, tc), lambda b, h: (b, h, 0)),
            pl.BlockSpec((1, LANES, tc), lambda b, h: (b, 0, 0)),
            pl.BlockSpec((1, dv), lambda b, h: (0, h)),
        ] + cast_in,
        out_specs=[pl.BlockSpec((1, t, dv), lambda b, h: (b, 0, h))] + cast_out,
        out_shape=[jax.ShapeDtypeStruct((batch, t, d), BF16)] + cast_shape,
        scratch_shapes=[
            pltpu.VMEM((2, nc, dv + LANES, dk), BF16),
            pltpu.VMEM((2, nc, 8, LANES), F32),
        ],
        compiler_params=_params(("arbitrary", "arbitrary")),
        name="mlstm_scan",
    )(main3, main3, tr, tr, pcol, prow, main_c3, tr_c, prowc, norm_g.reshape(1, d), cast[0], cast[1])


def _outproj_kernel(y_ref, x_ref, mod_ref, w_ref, cast_in_ref, cast_out_ref, o_ref, w_in_ref, w_out_ref):
    _cast_blocks(cast_in_ref, cast_out_ref, w_in_ref, w_out_ref)
    gate = mod_ref[0][2:3]
    o_ref[...] = x_ref[...] + gate * jnp.dot(y_ref[...], w_ref[...], preferred_element_type=F32)


def _outproj(y, x, mod3, w, rows_per_mod, tm, cast):
    m, d = x.shape
    tpm = rows_per_mod // tm
    cast_in, cast_out, cast_shape = _cast_plan(*cast, m // tm, lambda i: i)
    return pl.pallas_call(
        _outproj_kernel,
        grid=(m // tm,),
        in_specs=[
            pl.BlockSpec((tm, d), lambda i: (i, 0)),
            pl.BlockSpec((tm, d), lambda i: (i, 0)),
            pl.BlockSpec((1, 3, d), lambda i: (i // tpm, 0, 0)),
            pl.BlockSpec((d, d), lambda i: (0, 0)),
        ] + cast_in,
        out_specs=[pl.BlockSpec((tm, d), lambda i: (i, 0))] + cast_out,
        out_shape=[jax.ShapeDtypeStruct((m, d), F32)] + cast_shape,
        compiler_params=_params(("arbitrary",)),
        name="mlstm_out",
    )(y, x, mod3, w, cast[0], cast[1])


def _conv_kernel(x_ref, g_ref, mod_ref, wb_ref, wc_ref, wu_ref, cw_ref, wo_ref, o_ref, xn_ref, *, tn):
    j = pl.program_id(1)
    nj = pl.num_programs(1)
    mod = mod_ref[0]
    shift, gate = mod[0:1], mod[2:3]
    gain = g_ref[...] * (1.0 + mod[1:2])
    d = o_ref.shape[1]
    rg = x_ref.shape[0] // ROW_GROUPS
    cw = cw_ref[...]
    pos = lax.broadcasted_iota(jnp.int32, (rg, tn), 0) % GRID_W

    def step(first, last):
        acts = []
        for r in range(ROW_GROUPS):
            rows = slice(r * rg, (r + 1) * rg)
            if first:
                xn = _norm_rows(x_ref, r * rg, rg, gain, shift)
                xn_ref[rows, :] = xn
            else:
                xn = xn_ref[rows, :]
            bg = jnp.dot(xn, wb_ref[...], preferred_element_type=F32)
            cg = jnp.dot(xn, wc_ref[...], preferred_element_type=F32)
            u = jnp.dot(xn, wu_ref[...], preferred_element_type=F32)
            z = cg * u
            z_prev = jnp.where(pos == 0, 0.0, pltpu.roll(z, 1, 0))
            z_next = jnp.where(pos == GRID_W - 1, 0.0, pltpu.roll(z, rg - 1, 0))
            zc = cw[0:1] * z_prev + cw[1:2] * z + cw[2:3] * z_next
            acts.append((bg * zc).astype(BF16))
        for r in range(ROW_GROUPS):
            rows = slice(r * rg, (r + 1) * rg)
            for n0 in range(0, d, OUT_TN):
                cols = slice(n0, n0 + OUT_TN)
                upd = jnp.dot(acts[r], wo_ref[:, cols], preferred_element_type=F32)
                acc = upd if first else o_ref[rows, cols] + upd
                if last:
                    acc = x_ref[rows, cols] + gate[:, cols] * acc
                o_ref[rows, cols] = acc

    pl.when(j == 0)(lambda: step(True, False))
    pl.when((j > 0) & (j < nj - 1))(lambda: step(False, False))
    pl.when(j == nj - 1)(lambda: step(False, True))


def _conv(x, g, mod3, w3, cw, wo, rows_per_mod, tm):
    m, d = x.shape
    nj = d // CONV_TN
    tpm = rows_per_mod // tm
    return pl.pallas_call(
        functools.partial(_conv_kernel, tn=CONV_TN),
        grid=(m // tm, nj),
        in_specs=[
            pl.BlockSpec((tm, d), lambda i, j: (i, 0)),
            pl.BlockSpec((1, d), lambda i, j: (0, 0)),
            pl.BlockSpec((1, 3, d), lambda i, j: (i // tpm, 0, 0)),
            pl.BlockSpec((d, CONV_TN), lambda i, j: (0, j)),
            pl.BlockSpec((d, CONV_TN), lambda i, j: (0, nj + j)),
            pl.BlockSpec((d, CONV_TN), lambda i, j: (0, 2 * nj + j)),
            pl.BlockSpec((3, CONV_TN), lambda i, j: (0, j)),
            pl.BlockSpec((CONV_TN, d), lambda i, j: (j, 0)),
        ],
        out_specs=pl.BlockSpec((tm, d), lambda i, j: (i, 0)),
        out_shape=jax.ShapeDtypeStruct((m, d), F32),
        scratch_shapes=[pltpu.VMEM((tm, d), BF16)],
        compiler_params=_params(("parallel", "arbitrary")),
        name="conv_mixer",
    )(x, g.reshape(1, d), mod3, w3, w3, w3, cw, wo)


def kernel(x, c, ctx, c_ctx, w_mod, b_mod, norm_g, ffn_w_in, ffn_w_out, mlstm_w_in, mlstm_b_gate,
           mlstm_norm_g, mlstm_w_out, conv_w_in, conv_w, conv_w_out, final_norm_g):
    batch, t, d = x.shape
    tc = ctx.shape[1]
    depth = w_mod.shape[0]
    assert depth == 2 and d % (2 * HEADS * LANES) == 0 and t % GRID_W == 0
    dv = d // HEADS
    dk = dv // 2
    qk = HEADS * dk

    c_rows = jnp.concatenate([c, c_ctx[None, :], jnp.zeros((-(batch + 1) % 8, d), F32)], axis=0)
    mod = _modulation(c_rows, w_mod, b_mod).reshape(depth, c_rows.shape[0], N_MOD, d)

    def mod3(layer, sub, context=False):
        rows = mod[layer, batch:batch + 1] if context else mod[layer, :batch]
        return rows[:, 3 * sub:3 * sub + 3, :]

    tm = min(ROW_TM, t)
    tmix = min(MIX_TM, t)
    xf = x.reshape(batch * t, d)
    cf = ctx.reshape(batch * tc, d)

    w00_in, w00_out = _cast_ffn(ffn_w_in, ffn_w_out, 0, 0)
    xf = _ffn(xf, norm_g[0, 0], mod3(0, 0), w00_in, w00_out, t, tm)
    cf = _ffn(cf, norm_g[0, 0], mod3(0, 0, True), w00_in, w00_out, batch * tc, tm)

    w_in = mlstm_w_in[0]
    w_qt = (w_in[:, :qk] * (dk ** -0.5)).astype(BF16).T
    w_k = w_in[:, qk:2 * qk].astype(BF16)
    w_vt = w_in[:, 2 * qk:2 * qk + d].astype(BF16).T
    g0 = 2 * qk + d
    ng = 4 * HEADS
    w_gate = jnp.pad(w_in[:, g0:g0 + ng].astype(BF16), ((0, 0), (0, LANES - ng)))
    b_gate = jnp.pad(mlstm_b_gate[0], (0, LANES - ng)).reshape(1, LANES)
    w_o = w_in[:, g0 + ng:].astype(BF16)
    main, tr, gates = _proj(xf, norm_g[0, 1], mod3(0, 1), jnp.concatenate([w_k, w_o], axis=1),
                            jnp.concatenate([w_qt, w_vt], axis=0), w_gate, b_gate, t, min(PROJ_TM, t))
    main_c, tr_c, gates_c = _proj(cf, norm_g[0, 1], mod3(0, 1, True), w_k, w_vt, w_gate, b_gate,
                                  tc, min(PROJ_TM, tc))
    pcol, prow = _gateprep(gates, batch, SCAN_L)
    _, prowc = _gateprep(gates_c, batch, SCAN_L)
    y, w01_in, w01_out = _scan(main, tr, pcol, prow, main_c, tr_c, prowc, mlstm_norm_g[0], batch, t, tc, d,
                               cast=(ffn_w_in, ffn_w_out, 0, 1))
    xf, w10_in, w10_out = _outproj(y.reshape(batch * t, d), xf, mod3(0, 1), mlstm_w_out[0].astype(BF16), t, tmix,
                                   cast=(ffn_w_in, ffn_w_out, 1, 0))

    xf = _ffn(xf, norm_g[0, 2], mod3(0, 2), w01_in, w01_out, t, tm)

    xf = _ffn(xf, norm_g[1, 0], mod3(1, 0), w10_in, w10_out, t, tm)
    xf = _conv(xf, norm_g[1, 1], mod3(1, 1), conv_w_in[0].astype(BF16), conv_w[0],
               conv_w_out[0].astype(BF16), t, tm)
    w11_in, w11_out = _cast_ffn(ffn_w_in, ffn_w_out, 1, 1)
    xf = _ffn(xf, norm_g[1, 2], mod3(1, 2), w11_in, w11_out, t, tm, final_g=final_norm_g)
    return xf.reshape(batch, t, d)
```
